```python
import jax, jax.numpy as jnp
from jax import lax
import numpy as np

D_MODEL = 4096
BATCH = 32
SEQ = 256
DEPTH = 1
DEC_BATCH = 2
DEC_SEQ = 2048
PAST_LEN = 512

GRID_W = 64
D_A = D_MODEL // 2
HEAD_A = 64
H_A = D_A // HEAD_A
D_B = D_MODEL - D_A
G_B = 8
CH_B = D_B // G_B
CHUNK = 128
LORA_W = 64
LORA_A = 64
LORA_G = 256
C_SHIFT = 3 * D_A + 2 * LORA_W + 2 * LORA_A + LORA_G
D_IN = C_SHIFT + 2 * D_B
N_EXPERTS = 16
CAPACITY_FACTOR = 2
D_FF = 11 * D_MODEL // 8
N_MOD = 6
RMS_EPS = 1e-6
GN_EPS = 64e-5
LN_EPS = 1e-5

kernel_name = "hybrid_rwkv7_gmlp_ecmoe_diffusion_step"


def rmsnorm(x, g):
    xf = x.astype(jnp.float32)
    y = xf * lax.rsqrt(jnp.mean(xf * xf, axis=-1, keepdims=True) + RMS_EPS)
    return (y * g.astype(jnp.float32)).astype(x.dtype)


def short_conv(z, w, latent):
    B, T, C = z.shape
    if latent:
        rows = T // GRID_W
        img = z.reshape(B, rows, GRID_W, C)
        out = lax.conv_general_dilated(img, w[:, :, None, :].astype(z.dtype), (1, 1), 'SAME',
                                       dimension_numbers=('NHWC', 'HWIO', 'NHWC'),
                                       feature_group_count=C)
        return out.reshape(B, T, C)
    return lax.conv_general_dilated(z, w[1][:, None, :].astype(z.dtype), (1,), 'SAME',
                                    dimension_numbers=('NWC', 'WIO', 'NWC'),
                                    feature_group_count=C)


def wkv_scan(S0, r, w, k, v, kk, b, reverse):
    def step(S, inp):
        r_t, w_t, k_t, v_t, kk_t, b_t = inp
        sa = jnp.einsum('bhvk,bhk->bhv', S, kk_t)
        S = S * w_t[:, :, None, :] - sa[..., None] * b_t[:, :, None, :] + v_t[..., None] * k_t[:, :, None, :]
        y = jnp.einsum('bhvk,bhk->bhv', S, r_t)
        return S, y
    xs = tuple(jnp.moveaxis(t, 1, 0) for t in (r, w, k, v, kk, b))
    S_fin, ys = lax.scan(step, S0.astype(jnp.float32), xs, reverse=reverse)
    return jnp.moveaxis(ys, 0, 1), S_fin


def rwkv_mixer(zs, S0, w0, w_lora_up, a0, a_up, g_up, k_k, k_a, r_k, ln_x_g, ln_x_b):
    B, T, _ = zs.shape
    f32 = jnp.float32
    r, k, v, wd, ad, gd = jnp.split(
        zs, [D_A, 2 * D_A, 3 * D_A, 3 * D_A + 2 * LORA_W, 3 * D_A + 2 * LORA_W + 2 * LORA_A], axis=-1)
    wd = wd.reshape(B, T, 2, LORA_W)
    ad = ad.reshape(B, T, 2, LORA_A)
    w_log = -jax.nn.softplus(-(w0 + jnp.einsum('btdl,dlc->btdc', jnp.tanh(wd), w_lora_up)).astype(f32)) - 0.5
    decay = jnp.exp(-jnp.exp(w_log))
    a = jax.nn.sigmoid((a0 + jnp.einsum('btdl,dlc->btdc', ad, a_up)).astype(f32))
    g = (jax.nn.sigmoid(gd) @ g_up).astype(f32)
    r = r.astype(f32); k = k.astype(f32); v = v.astype(f32)
    heads = lambda t: t.reshape(B, T, H_A, HEAD_A)
    kk = heads(k * k_k.astype(f32))
    kk = kk / jnp.maximum(jnp.sqrt(jnp.sum(kk * kk, axis=-1, keepdims=True)), 1e-12)
    k_dir = k[:, :, None, :] * (1.0 + (a - 1.0) * k_a.astype(f32))
    rh, vh = heads(r), heads(v)
    y_f, S_f = wkv_scan(S0[:, 0], rh, heads(decay[:, :, 0]), heads(k_dir[:, :, 0]), vh,
                        kk, kk * heads(a[:, :, 0]), reverse=False)
    y_b, S_b = wkv_scan(S0[:, 1], rh, heads(decay[:, :, 1]), heads(k_dir[:, :, 1]), vh,
                        kk, kk * heads(a[:, :, 1]), reverse=True)
    y = y_f + y_b
    mean = jnp.mean(y, axis=-1, keepdims=True)
    var = jnp.mean(jnp.square(y - mean), axis=-1, keepdims=True)
    y = ((y - mean) * lax.rsqrt(var + GN_EPS)).reshape(B, T, D_A) * ln_x_g.astype(f32) + ln_x_b.astype(f32)
    bonus = jnp.sum(rh * heads((k_dir[:, :, 0] + k_dir[:, :, 1]) * r_k.astype(f32)), axis=-1, keepdims=True) * vh
    y = (y + bonus.reshape(B, T, D_A)) * g
    return y.astype(zs.dtype), jnp.stack([S_f, S_b], axis=1)


def sgu_mixer(zu, zv, ln_g, ln_b, w_s, b_s):
    B, T, _ = zu.shape
    u = jax.nn.gelu(zu, approximate=False)
    vf = jax.nn.gelu(zv, approximate=False).reshape(B, T, G_B, CH_B).astype(jnp.float32)
    mean = jnp.mean(vf, axis=-1, keepdims=True)
    var = jnp.mean(jnp.square(vf - mean), axis=-1, keepdims=True)
    vn = ((vf - mean) * lax.rsqrt(var + LN_EPS)).astype(zv.dtype) * ln_g.reshape(G_B, CH_B) + ln_b.reshape(G_B, CH_B)
    vc = vn.reshape(B, T // CHUNK, CHUNK, G_B, CH_B)
    mixed = jnp.einsum('gpq,bnqgc->bnpgc', w_s, vc) + b_s.T[:, :, None]
    return u * mixed.reshape(B, T, D_B)


def moe_expert_choice(x, w_router, w_gate, w_up, w_down):
    T, D = x.shape
    cap = CAPACITY_FACTOR * T // N_EXPERTS
    probs = jax.nn.softmax((x @ w_router).astype(jnp.float32), axis=-1)
    gates, idx = lax.top_k(probs.T, cap)
    xe = jnp.take(x, idx, axis=0)
    h = jax.nn.silu(jnp.einsum('ecd,edf->ecf', xe, w_gate)) * jnp.einsum('ecd,edf->ecf', xe, w_up)
    ye = jnp.einsum('ecf,efd->ecd', h, w_down) * gates[..., None].astype(x.dtype)
    return jnp.zeros_like(x).at[idx.reshape(-1)].add(ye.reshape(-1, D))


def layer(x, mod, S0, lp, latent):
    shift1, scale1, gate1, shift2, scale2, gate2 = jnp.split(mod[:, None, :], N_MOD, axis=-1)
    h = rmsnorm(x, lp['norm1_g']) * (1 + scale1) + shift1
    z = h @ lp['w_in']
    zs = short_conv(z[..., :C_SHIFT], lp['conv_w'], latent)
    y_a, S = rwkv_mixer(zs, S0, lp['w0'], lp['w_lora_up'], lp['a0'], lp['a_up'], lp['g_up'],
                        lp['k_k'], lp['k_a'], lp['r_k'], lp['ln_x_g'], lp['ln_x_b'])
    y_b = sgu_mixer(z[..., C_SHIFT:C_SHIFT + D_B], z[..., C_SHIFT + D_B:],
                    lp['sgu_ln_g'], lp['sgu_ln_b'], lp['sgu_w'], lp['sgu_b'])
    x = x + gate1 * (jnp.concatenate([y_a, y_b], axis=-1) @ lp['w_out'])
    h = rmsnorm(x, lp['norm2_g']) * (1 + scale2) + shift2
    B, T, D = h.shape
    moe = moe_expert_choice(h.reshape(B * T, D), lp['w_router'], lp['w_exp_gate'],
                            lp['w_exp_up'], lp['w_exp_down']).reshape(B, T, D)
    x = x + gate2 * moe
    return x, S


def setup_inputs(seed: int = 0) -> dict:
    key = jax.random.key(seed)
    ks = jax.random.split(key, 32)
    nrm = lambda k, shape, s: jax.random.normal(k, shape, jnp.float32) * s
    L, D = DEPTH, D_MODEL
    return {
        'x_prompt': nrm(ks[0], (BATCH, SEQ, D), 1.0),
        'x_sample': nrm(ks[1], (DEC_BATCH, DEC_SEQ, D), 1.0),
        'state_wkv': nrm(ks[2], (DEC_BATCH, L, 2, H_A, HEAD_A, HEAD_A), 0.3),
        'c': nrm(ks[3], (DEC_BATCH, D), 1.0),
        'c_ctx': nrm(ks[4], (D,), 1.0),
        'norm1_g': 1.0 + nrm(ks[5], (L, D), 0.02),
        'norm2_g': 1.0 + nrm(ks[6], (L, D), 0.02),
        'w_mod': nrm(ks[7], (L, D, N_MOD * D), D ** -0.5),
        'b_mod': nrm(ks[8], (L, N_MOD * D), 0.02),
        'w_in': nrm(ks[9], (L, D, D_IN), D ** -0.5),
        'conv_w': nrm(ks[10], (L, 3, 3, C_SHIFT), 1.0 / 3.0),
        'w0': jax.random.uniform(ks[11], (L, 2, D_A), jnp.float32, minval=-6.0, maxval=-0.5),
        'w_lora_up': nrm(ks[12], (L, 2, LORA_W, D_A), LORA_W ** -0.5),
        'a0': nrm(ks[13], (L, 2, D_A), 0.5),
        'a_up': nrm(ks[14], (L, 2, LORA_A, D_A), LORA_A ** -0.5),
        'g_up': nrm(ks[15], (L, LORA_G, D_A), LORA_G ** -0.5),
        'k_k': 0.85 + nrm(ks[16], (L, D_A), 0.02),
        'k_a': 1.0 + nrm(ks[17], (L, D_A), 0.02),
        'r_k': nrm(ks[18], (L, D_A), 0.1),
        'ln_x_g': 1.0 + nrm(ks[19], (L, D_A), 0.02),
        'ln_x_b': nrm(ks[20], (L, D_A), 0.02),
        'sgu_ln_g': 1.0 + nrm(ks[21], (L, D_B), 0.02),
        'sgu_ln_b': nrm(ks[22], (L, D_B), 0.02),
        'sgu_w': nrm(ks[23], (L, G_B, CHUNK, CHUNK), CHUNK ** -0.5),
        'sgu_b': 1.0 + nrm(ks[24], (L, G_B, CHUNK), 0.1),
        'w_out': nrm(ks[25], (L, D, D), D ** -0.5),
        'w_router': nrm(ks[26], (L, D, N_EXPERTS), D ** -0.5),
        'w_exp_gate': nrm(ks[27], (L, N_EXPERTS, D, D_FF), D ** -0.5),
        'w_exp_up': nrm(ks[28], (L, N_EXPERTS, D, D_FF), D ** -0.5),
        'w_exp_down': nrm(ks[29], (L, N_EXPERTS, D_FF, D), D_FF ** -0.5),
        'final_g': 1.0 + nrm(ks[30], (D,), 0.02),
    }


def reference(x_prompt, x_sample, state_wkv, c, c_ctx, norm1_g, norm2_g, w_mod, b_mod, w_in,
              conv_w, w0, w_lora_up, a0, a_up, g_up, k_k, k_a, r_k, ln_x_g, ln_x_b,
              sgu_ln_g, sgu_ln_b, sgu_w, sgu_b, w_out, w_router, w_exp_gate, w_exp_up,
              w_exp_down, final_g):
    def layer_params(l):
        return {'norm1_g': norm1_g[l], 'norm2_g': norm2_g[l], 'w_in': w_in[l], 'conv_w': conv_w[l],
                'w0': w0[l], 'w_lora_up': w_lora_up[l], 'a0': a0[l], 'a_up': a_up[l],
                'g_up': g_up[l], 'k_k': k_k[l], 'k_a': k_a[l], 'r_k': r_k[l],
                'ln_x_g': ln_x_g[l], 'ln_x_b': ln_x_b[l], 'sgu_ln_g': sgu_ln_g[l],
                'sgu_ln_b': sgu_ln_b[l], 'sgu_w': sgu_w[l], 'sgu_b': sgu_b[l], 'w_out': w_out[l],
                'w_router': w_router[l], 'w_exp_gate': w_exp_gate[l], 'w_exp_up': w_exp_up[l],
                'w_exp_down': w_exp_down[l]}

    x = x_prompt
    new_states = []
    for l in range(DEPTH):
        lp = layer_params(l)
        mod_ctx = (jax.nn.silu(c_ctx) @ w_mod[l] + b_mod[l])[None, :]
        S0 = jnp.zeros((x.shape[0], 2, H_A, HEAD_A, HEAD_A), jnp.float32)
        x, S = layer(x, mod_ctx, S0, lp, latent=False)
        new_states.append(S.astype(x_prompt.dtype))
    y_prompt = rmsnorm(x, final_g)
    state_wkv_new = jnp.stack(new_states, axis=1)

    x = x_sample
    for l in range(DEPTH):
        lp = layer_params(l)
        mod = jax.nn.silu(c) @ w_mod[l] + b_mod[l]
        x, _ = layer(x, mod, state_wkv[:, l], lp, latent=True)
    y_sample = rmsnorm(x, final_g)
    return (y_prompt, y_sample, state_wkv_new)
```

```python
import functools

import jax
import jax.numpy as jnp
from jax import lax
from jax.experimental import pallas as pl
from jax.experimental.pallas import tpu as pltpu

F32 = jnp.float32
BF16 = jnp.bfloat16

D_MODEL = 4096
D_A = 2048
HEAD_A = 64
H_A = 32
D_B = 2048
G_B = 8
CH_B = 256
SGU_CHUNK = 128
LORA_W = 64
LORA_A = 64
LORA_G = 256
C_SHIFT = 3 * D_A + 2 * LORA_W + 2 * LORA_A + LORA_G
D_IN = C_SHIFT + 2 * D_B
N_EXPERTS = 16
CAPACITY_FACTOR = 2
D_FF = 5632
N_MOD = 6
GRID_W = 64
RMS_EPS = 1e-6
GN_EPS = 64e-5
LN_EPS = 1e-5

V7X_LANES = 128
V7X_SUBLANES = 8
V7X_VMEM_BYTES = 64 * 1024 * 1024
VMEM_LIMIT_BIG = 52 * 1024 * 1024
VMEM_LIMIT_MID = 40 * 1024 * 1024

WKV_CHUNK = 64
PAIR = 2 * HEAD_A
WKV_GW = 512
N_PAIRS_TOTAL = D_A // PAIR


def _cparams(sem, vmem=None):
    return pltpu.CompilerParams(dimension_semantics=sem, vmem_limit_bytes=vmem)


def _bdot(a, b):
    return jnp.dot(a.astype(BF16), b.astype(BF16), preferred_element_type=F32)


def _nt(a, b):
    return lax.dot_general(a.astype(BF16), b.astype(BF16), (((1,), (1,)), ((), ())),
                           preferred_element_type=F32)


def _tn(a, b):
    return lax.dot_general(a.astype(BF16), b.astype(BF16), (((0,), (0,)), ((), ())),
                           preferred_element_type=F32)


def _split2(x):
    hi = x.astype(BF16)
    lo = (x - hi.astype(F32)).astype(BF16)
    return hi, lo


def _split3(x):
    hi = x.astype(BF16)
    r1 = x - hi.astype(F32)
    mid = r1.astype(BF16)
    lo = (r1 - mid.astype(F32)).astype(BF16)
    return hi, mid, lo


def _dot3(a, b):
    ah, al = _split2(a)
    bh, bl = _split2(b)
    d = functools.partial(jnp.dot, preferred_element_type=F32)
    return d(ah, bh) + (d(al, bh) + d(ah, bl))


def _dot_exact_lhs(a_bf, b):
    bh, bm, bl = _split3(b)
    d = functools.partial(jnp.dot, preferred_element_type=F32)
    return d(a_bf, bh) + (d(a_bf, bm) + d(a_bf, bl))


def _dot_exact_rhs(a, b_bf):
    ah, am, al = _split3(a)
    d = functools.partial(jnp.dot, preferred_element_type=F32)
    return d(ah, b_bf) + (d(am, b_bf) + d(al, b_bf))


def _sigmoid(x):
    return jax.nn.sigmoid(x)


def _mod_kernel(c_ref, w_ref, b_ref, o_ref):
    cv = c_ref[...]
    s = cv * _sigmoid(cv)
    o_ref[...] = _dot3(s, w_ref[...]) + b_ref[...]


def _mod_call(cvec, w_mod, b_mod):
    rows, d = cvec.shape
    n = w_mod.shape[1]
    tn = 512
    return pl.pallas_call(
        _mod_kernel,
        out_shape=jax.ShapeDtypeStruct((rows, n), F32),
        grid=(n // tn,),
        in_specs=[pl.BlockSpec((rows, d), lambda j: (0, 0)),
                  pl.BlockSpec((d, tn), lambda j: (0, j)),
                  pl.BlockSpec((1, tn), lambda j: (0, j))],
        out_specs=pl.BlockSpec((rows, tn), lambda j: (0, j)),
        compiler_params=_cparams(("parallel",), VMEM_LIMIT_MID),
        name="mod_proj",
    )(cvec, w_mod, b_mod)


def _inproj_kernel(x_ref, g_ref, sc_ref, sh_ref, w_ref, o_ref, h_scr):
    @pl.when(pl.program_id(1) == 0)
    def _():
        x = x_ref[...]
        ms = jnp.mean(x * x, axis=-1, keepdims=True)
        y = x * lax.rsqrt(ms + RMS_EPS) * g_ref[...]
        h_scr[...] = (y * (1.0 + sc_ref[0]) + sh_ref[0]).astype(BF16)

    o_ref[...] = jnp.dot(h_scr[...], w_ref[...], preferred_element_type=F32)


def _mod_row_map(tm, rows_per_mod, mod_base, comp):
    def index_map(i, *_):
        return ((mod_base + (i * tm) // rows_per_mod) * N_MOD + comp, 0, 0)
    return index_map


def _inproj_call(x, norm_g, modr, w_bf, rows_per_mod, mod_base):
    tk, d = x.shape
    n = w_bf.shape[1]
    tm, tn = 512, 768
    return pl.pallas_call(
        _inproj_kernel,
        out_shape=jax.ShapeDtypeStruct((tk, n), F32),
        grid=(tk // tm, n // tn),
        in_specs=[pl.BlockSpec((tm, d), lambda i, j: (i, 0)),
                  pl.BlockSpec((1, d), lambda i, j: (0, 0)),
                  pl.BlockSpec((1, 1, d), _mod_row_map(tm, rows_per_mod, mod_base, 1)),
                  pl.BlockSpec((1, 1, d), _mod_row_map(tm, rows_per_mod, mod_base, 0)),
                  pl.BlockSpec((d, tn), lambda i, j: (0, j))],
        out_specs=pl.BlockSpec((tm, tn), lambda i, j: (i, j)),
        scratch_shapes=[pltpu.VMEM((tm, d), BF16)],
        compiler_params=_cparams(("parallel", "arbitrary"), VMEM_LIMIT_BIG),
        name="in_proj",
    )(x, norm_g, modr, modr, w_bf)


CONV_PAD = 72
CONV_ROWS = 256


def _conv_kernel(z_ref, w_ref, o_ref, ext, *, seq, width, vertical):
    bb = z_ref.shape[0]
    tc = z_ref.shape[2]
    zeros = jnp.zeros((CONV_PAD, tc), F32)
    ext[pl.ds(0, CONV_PAD), :] = zeros
    ext[pl.ds(CONV_PAD + seq, CONV_PAD), :] = zeros
    rows = min(CONV_ROWS, seq)
    col = lax.broadcasted_iota(jnp.int32, (rows, tc), 0)
    for b in range(bb):
        ext[pl.ds(CONV_PAD, seq), :] = z_ref[b]
        for r0 in range(0, seq, rows):
            cpos = (col + r0) % width
            acc = jnp.zeros((rows, tc), F32)
            for di in ((-1, 0, 1) if vertical else (0,)):
                for dj in (-1, 0, 1):
                    tap = ext[pl.ds(CONV_PAD + r0 + di * width + dj, rows), :]
                    if dj == -1:
                        tap = jnp.where(cpos == 0, 0.0, tap)
                    elif dj == 1:
                        tap = jnp.where(cpos == width - 1, 0.0, tap)
                    acc = acc + tap * w_ref[di + 1, pl.ds(dj + 1, 1), :]
            o_ref[b, pl.ds(r0, rows), :] = acc


def _conv_call(z3, conv_w, latent):
    nb, seq, _ = z3.shape
    if latent:
        bb, tc, width = 1, 256, GRID_W
    else:
        bb, tc, width = 4, 512, seq
    kern = functools.partial(_conv_kernel, seq=seq, width=width, vertical=latent)
    return pl.pallas_call(
        kern,
        out_shape=jax.ShapeDtypeStruct((nb, seq, C_SHIFT), F32),
        grid=(nb // bb, C_SHIFT // tc),
        in_specs=[pl.BlockSpec((bb, seq, tc), lambda b, j: (b, 0, j)),
                  pl.BlockSpec((3, 3, tc), lambda b, j: (0, 0, j))],
        out_specs=pl.BlockSpec((bb, seq, tc), lambda b, j: (b, 0, j)),
        scratch_shapes=[pltpu.VMEM((seq + 2 * CONV_PAD, tc), F32)],
        compiler_params=_cparams(("parallel", "parallel"), VMEM_LIMIT_MID),
        name="short_conv",
    )(z3, conv_w)


def _prep_kernel(zl_ref, w0_ref, wup_ref, a0_ref, aup_ref, gup_ref, lw_ref, a_ref, g_ref):
    zl = zl_ref[...]
    tm = zl.shape[0]
    wd = jnp.tanh(zl[:, 0:2 * LORA_W])
    ad = zl[:, 2 * LORA_W:2 * LORA_W + 2 * LORA_A]
    gd = _sigmoid(zl[:, 2 * LORA_W + 2 * LORA_A:])
    lane = lax.broadcasted_iota(jnp.int32, (tm, 2 * LORA_W), 1)
    for d in range(2):
        sel = (lane < LORA_W) if d == 0 else (lane >= LORA_W)
        wlin = w0_ref[pl.ds(d, 1), :] + _dot3(jnp.where(sel, wd, 0.0), wup_ref[...])
        w_log = -jax.nn.softplus(-wlin) - 0.5
        lw_ref[d] = -jnp.exp(w_log)
        alin = a0_ref[pl.ds(d, 1), :] + _dot3(jnp.where(sel, ad, 0.0), aup_ref[...])
        a_ref[d] = _sigmoid(alin)
    g_ref[...] = _dot3(gd, gup_ref[...])


def _prep_call(zs, w0, wup, a0, aup, gup):
    tk = zs.shape[0]
    tm = 256
    lblk = (3 * D_A) // 512
    return pl.pallas_call(
        _prep_kernel,
        out_shape=[jax.ShapeDtypeStruct((2, tk, D_A), F32),
                   jax.ShapeDtypeStruct((2, tk, D_A), F32),
                   jax.ShapeDtypeStruct((tk, D_A), F32)],
        grid=(tk // tm,),
        in_specs=[pl.BlockSpec((tm, 512), lambda i: (i, lblk)),
                  pl.BlockSpec((2, D_A), lambda i: (0, 0)),
                  pl.BlockSpec((2 * LORA_W, D_A), lambda i: (0, 0)),
                  pl.BlockSpec((2, D_A), lambda i: (0, 0)),
                  pl.BlockSpec((2 * LORA_A, D_A), lambda i: (0, 0)),
                  pl.BlockSpec((LORA_G, D_A), lambda i: (0, 0))],
        out_specs=[pl.BlockSpec((2, tm, D_A), lambda i: (0, i, 0)),
                   pl.BlockSpec((2, tm, D_A), lambda i: (0, i, 0)),
                   pl.BlockSpec((tm, D_A), lambda i: (i, 0))],
        compiler_params=_cparams(("parallel",), VMEM_LIMIT_MID),
        name="rwkv_prep",
    )(zs, w0, wup, a0, aup, gup)


def _wkv_kernel(r_ref, k_ref, v_ref, lw_ref, a_ref, kk_ref, ka_ref, s0_ref, y_ref, sout_ref, s_scr):
    d = pl.program_id(1)
    ci = pl.program_id(3)
    nci = pl.num_programs(3)
    C = r_ref.shape[0]
    gw = r_ref.shape[1]
    n_pairs = gw // PAIR

    @pl.when(ci == 0)
    def _():
        s_scr[...] = s0_ref[0, 0]

    rev = d == 1
    ri = lax.broadcasted_iota(jnp.int32, (C, C), 0)
    cj = lax.broadcasted_iota(jnp.int32, (C, C), 1)
    before = jnp.where(rev, cj - ri, ri - cj) >= 0
    tri = jnp.where(before, 1.0, 0.0).astype(BF16)

    lw = lw_ref[0]
    cum = _dot_exact_lhs(tri, lw)
    total = jnp.where(rev, cum[0:1, :], cum[C - 1:C, :])

    r2 = lax.broadcasted_iota(jnp.int32, (2 * C, 2 * C), 0)
    c2 = lax.broadcasted_iota(jnp.int32, (2 * C, 2 * C), 1)
    same = (r2 // C) == (c2 // C)
    dd = jnp.where(rev, (c2 % C) - (r2 % C), (r2 % C) - (c2 % C))
    strict = same & (dd > 0)
    incl = same & (dd >= 0)
    eye = jnp.where(r2 == c2, 1.0, 0.0)
    merge_masks = []
    s = 1
    while s < C:
        merge_masks.append(((r2 // (2 * s)) == (c2 // (2 * s))) & ((r2 // s) != (c2 // s)))
        s *= 2
    lane = lax.broadcasted_iota(jnp.int32, (C, PAIR), 1)
    first = lane < HEAD_A
    hr = lax.broadcasted_iota(jnp.int32, (PAIR, PAIR), 0) // HEAD_A
    hc = lax.broadcasted_iota(jnp.int32, (PAIR, PAIR), 1) // HEAD_A
    head_ones = jnp.where(hr == hc, 1.0, 0.0).astype(BF16)

    def stack(x):
        return jnp.concatenate([jnp.where(first, x, 0.0), jnp.where(first, 0.0, x)], axis=0)

    for p in range(n_pairs):
        sl = slice(p * PAIR, (p + 1) * PAIR)
        r = r_ref[:, sl]
        k = k_ref[:, sl]
        v = v_ref[:, sl]
        a = a_ref[0, :, sl]
        lwp = lw[:, sl]
        cp = cum[:, sl]
        tot = total[:, sl]
        kkr = k * kk_ref[:, sl]
        n2 = _dot_exact_rhs(kkr * kkr, head_ones)
        kk = kkr / jnp.maximum(jnp.sqrt(n2), 1e-12)
        kd = k * (1.0 + (a - 1.0) * ka_ref[:, sl])
        b = kk * a
        e_in = jnp.exp(cp)
        e_ex = jnp.exp(cp - lwp)
        e_ng = jnp.exp(-cp)
        e_tot = jnp.exp(tot - cp)
        s_a = stack(kk * e_ex)
        s_r = stack(r * e_in)
        s_b = stack(b * e_ng)
        s_k = stack(kd * e_ng)
        s_kp = stack(kd * e_tot)
        s_bp = stack(b * e_tot)
        s_v = stack(v)

        g = _nt(jnp.concatenate([s_a, s_r], axis=0), jnp.concatenate([s_b, s_k], axis=0))
        n = jnp.where(strict, -g[0:2 * C, 0:2 * C], 0.0)
        m_ak = jnp.where(strict, g[0:2 * C, 2 * C:], 0.0)
        m_rb = jnp.where(incl, g[2 * C:, 0:2 * C], 0.0)
        m_rk = jnp.where(incl, g[2 * C:, 2 * C:], 0.0)

        t = eye + jnp.where(merge_masks[0], n, 0.0)
        for mask in merge_masks[1:]:
            t = t + _bdot(t, _bdot(jnp.where(mask, n, 0.0), t))

        w = _bdot(jnp.concatenate([m_ak, m_rk], axis=0), s_v)
        ta = _bdot(t, jnp.concatenate([s_a, w[0:2 * C]], axis=1))
        s_at = ta[:, 0:PAIR]
        u_hat = ta[:, PAIR:]

        s_old = s_scr[p]
        z = _nt(jnp.concatenate([s_at, s_r], axis=0), s_old)
        u = z[0:2 * C] + u_hat
        y_st = z[2 * C:] + w[2 * C:] - _bdot(m_rb, u)
        y_ref[0, :, sl] = y_st[0:C] + y_st[C:]
        s_new = s_old * jnp.exp(tot) + _tn(jnp.concatenate([s_v, u], axis=0),
                                           jnp.concatenate([s_kp, -s_bp], axis=0))
        s_scr[p] = s_new

    @pl.when(ci == nci - 1)
    def _():
        sout_ref[0, 0] = s_scr[...]


def _wkv_call(zs, lw, a, k_k, k_a, s0_bd, nb, seq):
    tk = zs.shape[0]
    C = WKV_CHUNK
    nch = seq // C
    gw = WKV_GW
    ng = D_A // gw
    npair = gw // PAIR
    koff = D_A // gw

    def tokmap(b, d, g, c):
        return b * nch + c + d * (nch - 1 - 2 * c)

    return pl.pallas_call(
        _wkv_kernel,
        out_shape=[jax.ShapeDtypeStruct((2, tk, D_A), F32),
                   jax.ShapeDtypeStruct((nb, 2, N_PAIRS_TOTAL, PAIR, PAIR), F32)],
        grid=(nb, 2, ng, nch),
        in_specs=[pl.BlockSpec((C, gw), lambda b, d, g, c: (tokmap(b, d, g, c), g)),
                  pl.BlockSpec((C, gw), lambda b, d, g, c: (tokmap(b, d, g, c), koff + g)),
                  pl.BlockSpec((C, gw), lambda b, d, g, c: (tokmap(b, d, g, c), 2 * koff + g)),
                  pl.BlockSpec((1, C, gw), lambda b, d, g, c: (d, tokmap(b, d, g, c), g)),
                  pl.BlockSpec((1, C, gw), lambda b, d, g, c: (d, tokmap(b, d, g, c), g)),
                  pl.BlockSpec((1, gw), lambda b, d, g, c: (0, g)),
                  pl.BlockSpec((1, gw), lambda b, d, g, c: (0, g)),
                  pl.BlockSpec((1, 1, npair, PAIR, PAIR), lambda b, d, g, c: (b, d, g, 0, 0))],
        out_specs=[pl.BlockSpec((1, C, gw), lambda b, d, g, c: (d, tokmap(b, d, g, c), g)),
                   pl.BlockSpec((1, 1, npair, PAIR, PAIR), lambda b, d, g, c: (b, d, g, 0, 0))],
        scratch_shapes=[pltpu.VMEM((npair, PAIR, PAIR), F32)],
        compiler_params=_cparams(("parallel", "parallel", "parallel", "arbitrary"), VMEM_LIMIT_MID),
        name="wkv_scan",
    )(zs, zs, zs, lw, a, k_k, k_a, s0_bd)


def _rwkv_out_kernel(y_ref, r_ref, k_ref, v_ref, a_ref, g_ref, ka_ref, rk_ref, lg_ref, lb_ref, o_ref):
    gw = r_ref.shape[1]
    hr = lax.broadcasted_iota(jnp.int32, (PAIR, PAIR), 0) // HEAD_A
    hc = lax.broadcasted_iota(jnp.int32, (PAIR, PAIR), 1) // HEAD_A
    head_ones = jnp.where(hr == hc, 1.0, 0.0).astype(BF16)
    for p in range(gw // PAIR):
        sl = slice(p * PAIR, (p + 1) * PAIR)
        y = y_ref[0, :, sl] + y_ref[1, :, sl]
        mean = _dot_exact_rhs(y, head_ones) * (1.0 / HEAD_A)
        yc = y - mean
        var = _dot_exact_rhs(yc * yc, head_ones) * (1.0 / HEAD_A)
        yn = yc * lax.rsqrt(var + GN_EPS) * lg_ref[:, sl] + lb_ref[:, sl]
        r = r_ref[:, sl]
        k = k_ref[:, sl]
        ka = ka_ref[:, sl]
        kd0 = k * (1.0 + (a_ref[0, :, sl] - 1.0) * ka)
        kd1 = k * (1.0 + (a_ref[1, :, sl] - 1.0) * ka)
        dotp = _dot_exact_rhs(r * ((kd0 + kd1) * rk_ref[:, sl]), head_ones)
        o_ref[:, sl] = ((yn + dotp * v_ref[:, sl]) * g_ref[:, sl]).astype(BF16)


def _rwkv_out_call(y2, zs, a, g, k_a, r_k, ln_g, ln_b):
    tk = zs.shape[0]
    tm, gw = 256, 512
    koff = D_A // gw
    vec = pl.BlockSpec((1, gw), lambda i, j: (0, j))
    return pl.pallas_call(
        _rwkv_out_kernel,
        out_shape=jax.ShapeDtypeStruct((tk, D_A), BF16),
        grid=(tk // tm, D_A // gw),
        in_specs=[pl.BlockSpec((2, tm, gw), lambda i, j: (0, i, j)),
                  pl.BlockSpec((tm, gw), lambda i, j: (i, j)),
                  pl.BlockSpec((tm, gw), lambda i, j: (i, koff + j)),
                  pl.BlockSpec((tm, gw), lambda i, j: (i, 2 * koff + j)),
                  pl.BlockSpec((2, tm, gw), lambda i, j: (0, i, j)),
                  pl.BlockSpec((tm, gw), lambda i, j: (i, j)),
                  vec, vec, vec, vec],
        out_specs=pl.BlockSpec((tm, gw), lambda i, j: (i, j)),
        compiler_params=_cparams(("parallel", "parallel"), VMEM_LIMIT_MID),
        name="rwkv_out",
    )(y2, zs, zs, zs, a, g, k_a, r_k, ln_g, ln_b)


def _gelu(x):
    return 0.5 * x * (1.0 + lax.erf(x * 0.7071067811865476))


def _sgu_kernel(*refs):
    nblk = (len(refs) - 5) // 2
    zu_refs, zv_refs = refs[:nblk], refs[nblk:2 * nblk]
    lg_ref, lb_ref, ws_ref, bs_ref, o_ref = refs[2 * nblk:]
    tm = o_ref.shape[0]
    per_blk = zu_refs[0].shape[1] // CH_B
    for grp in range(G_B):
        zu_ref, zv_ref = zu_refs[grp // per_blk], zv_refs[grp // per_blk]
        bsl = slice((grp % per_blk) * CH_B, (grp % per_blk + 1) * CH_B)
        sl = slice(grp * CH_B, (grp + 1) * CH_B)
        vf = _gelu(zv_ref[:, bsl])
        mean = jnp.mean(vf, axis=-1, keepdims=True)
        vc = vf - mean
        var = jnp.mean(vc * vc, axis=-1, keepdims=True)
        vn = vc * lax.rsqrt(var + LN_EPS) * lg_ref[:, sl] + lb_ref[:, sl]
        for c0 in range(0, tm, SGU_CHUNK):
            mixed = _bdot(ws_ref[grp], vn[c0:c0 + SGU_CHUNK]) + bs_ref[grp]
            u = _gelu(zu_ref[pl.ds(c0, SGU_CHUNK), bsl])
            o_ref[pl.ds(c0, SGU_CHUNK), sl] = (u * mixed).astype(BF16)


def _sgu_call(z, ln_g, ln_b, w_s, b_s3):
    tk = z.shape[0]
    tm = 256
    bw = 512
    u0 = C_SHIFT // bw
    v0 = (C_SHIFT + D_B) // bw
    nbw = D_B // bw
    in_specs = ([pl.BlockSpec((tm, bw), functools.partial(lambda i, q: (i, q), q=u0 + q)) for q in range(nbw)]
                + [pl.BlockSpec((tm, bw), functools.partial(lambda i, q: (i, q), q=v0 + q)) for q in range(nbw)]
                + [pl.BlockSpec((1, D_B), lambda i: (0, 0)),
                   pl.BlockSpec((1, D_B), lambda i: (0, 0)),
                   pl.BlockSpec((G_B, SGU_CHUNK, SGU_CHUNK), lambda i: (0, 0, 0)),
                   pl.BlockSpec((G_B, SGU_CHUNK, 1), lambda i: (0, 0, 0))])
    return pl.pallas_call(
        _sgu_kernel,
        out_shape=jax.ShapeDtypeStruct((tk, D_B), BF16),
        grid=(tk // tm,),
        in_specs=in_specs,
        out_specs=pl.BlockSpec((tm, D_B), lambda i: (i, 0)),
        compiler_params=_cparams(("parallel",), VMEM_LIMIT_MID),
        name="sgu",
    )(*([z] * (2 * nbw)), ln_g, ln_b, w_s, b_s3)


def _outproj_kernel(ya_ref, yb_ref, wa_ref, wb_ref, x_ref, gate_ref, o_ref):
    acc = jnp.dot(ya_ref[...], wa_ref[...], preferred_element_type=F32)
    acc = acc + jnp.dot(yb_ref[...], wb_ref[...], preferred_element_type=F32)
    o_ref[...] = x_ref[...] + gate_ref[0] * acc


def _outproj_call(ya, yb, w_bf, x, modr, rows_per_mod, mod_base):
    tk = x.shape[0]
    tm, tn = 512, 1024

    def gate_map(i, j):
        return ((mod_base + (i * tm) // rows_per_mod) * N_MOD + 2, 0, j)

    return pl.pallas_call(
        _outproj_kernel,
        out_shape=jax.ShapeDtypeStruct((tk, D_MODEL), F32),
        grid=(tk // tm, D_MODEL // tn),
        in_specs=[pl.BlockSpec((tm, D_A), lambda i, j: (i, 0)),
                  pl.BlockSpec((tm, D_B), lambda i, j: (i, 0)),
                  pl.BlockSpec((D_A, tn), lambda i, j: (0, j)),
                  pl.BlockSpec((D_B, tn), lambda i, j: (1, j)),
                  pl.BlockSpec((tm, tn), lambda i, j: (i, j)),
                  pl.BlockSpec((1, 1, tn), gate_map)],
        out_specs=pl.BlockSpec((tm, tn), lambda i, j: (i, j)),
        compiler_params=_cparams(("parallel", "arbitrary"), VMEM_LIMIT_MID),
        name="out_proj",
    )(ya, yb, w_bf, w_bf, x, modr)


def _norm2_kernel(x_ref, g_ref, sc_ref, sh_ref, wr_ref, h_ref, p_ref):
    x = x_ref[...]
    ms = jnp.mean(x * x, axis=-1, keepdims=True)
    y = x * lax.rsqrt(ms + RMS_EPS) * g_ref[...]
    h = y * (1.0 + sc_ref[0]) + sh_ref[0]
    h_ref[...] = h
    hh, hl = _split2(h)
    wh, wl = _split2(wr_ref[...])
    nt = functools.partial(lax.dot_general, dimension_numbers=(((1,), (1,)), ((), ())),
                           preferred_element_type=F32)
    logits = nt(wh, hh) + (nt(wh, hl) + nt(wl, hh))
    m = jnp.max(logits, axis=0, keepdims=True)
    e = jnp.exp(logits - m)
    p_ref[...] = e / jnp.sum(e, axis=0, keepdims=True)


def _norm2_call(x1, norm_g, modr, w_router_t, rows_per_mod, mod_base):
    tk = x1.shape[0]
    tm = 256
    return pl.pallas_call(
        _norm2_kernel,
        out_shape=[jax.ShapeDtypeStruct((tk, D_MODEL), F32),
                   jax.ShapeDtypeStruct((N_EXPERTS, tk), F32)],
        grid=(tk // tm,),
        in_specs=[pl.BlockSpec((tm, D_MODEL), lambda i: (i, 0)),
                  pl.BlockSpec((1, D_MODEL), lambda i: (0, 0)),
                  pl.BlockSpec((1, 1, D_MODEL), _mod_row_map(tm, rows_per_mod, mod_base, 4)),
                  pl.BlockSpec((1, 1, D_MODEL), _mod_row_map(tm, rows_per_mod, mod_base, 3)),
                  pl.BlockSpec((N_EXPERTS, D_MODEL), lambda i: (0, 0))],
        out_specs=[pl.BlockSpec((tm, D_MODEL), lambda i: (i, 0)),
                   pl.BlockSpec((N_EXPERTS, tm), lambda i: (0, i))],
        compiler_params=_cparams(("parallel",), VMEM_LIMIT_MID),
        name="norm2_router",
    )(x1, norm_g, modr, modr, w_router_t)


ROUTE_SLOTS = 128


def _route_kernel(p_ref, cnt_ref, slot_ref, *, cap):
    ne, nt = p_ref.shape
    cnt_scr, sel_scr = cnt_ref, slot_ref
    p = p_ref[...]
    bits = pltpu.bitcast(p, jnp.int32)

    def count_ge(v):
        return jnp.sum(jnp.where(bits >= v, 1.0, 0.0), axis=1, keepdims=True)

    def bis(it, cur):
        cand = cur | (jnp.int32(1) << (30 - it))
        return jnp.where(count_ge(cand) >= cap, cand, cur)

    thr = lax.fori_loop(0, 31, bis, jnp.zeros((ne, 1), jnp.int32))
    gt = bits > thr
    eq = bits == thr
    need = cap - jnp.sum(jnp.where(gt, 1.0, 0.0), axis=1, keepdims=True)

    lr = lax.broadcasted_iota(jnp.int32, (V7X_LANES, V7X_LANES), 0)
    lc = lax.broadcasted_iota(jnp.int32, (V7X_LANES, V7X_LANES), 1)
    upper = jnp.where(lr <= lc, 1.0, 0.0).astype(BF16)

    def prefix(mask_f32, dst):
        carry = jnp.zeros((ne, 1), F32)
        for c0 in range(0, nt, V7X_LANES):
            blk = jnp.dot(mask_f32[:, c0:c0 + V7X_LANES].astype(BF16), upper,
                          preferred_element_type=F32) + carry
            dst[:, c0:c0 + V7X_LANES] = blk
            carry = blk[:, V7X_LANES - 1:V7X_LANES]

    prefix(jnp.where(eq, 1.0, 0.0), cnt_scr)
    sel = gt | (eq & (cnt_scr[...] <= need))
    prefix(jnp.where(sel, 1.0, 0.0), cnt_scr)
    sel_scr[...] = jnp.where(sel, cnt_scr[...], -1.0)


def _slots_kernel(cnt_ref, slot_ref, p_ref, idx_ref, gate_ref):
    nt = cnt_ref.shape[2]
    sb = pl.program_id(1)
    slot = (lax.broadcasted_iota(jnp.int32, (ROUTE_SLOTS, V7X_LANES), 0) + sb * ROUTE_SLOTS).astype(F32)

    def chunk(c, accs):
        acc_i, acc_g = accs
        off = pl.multiple_of(c * V7X_LANES, V7X_LANES)
        c_row = cnt_ref[0, :, pl.ds(off, V7X_LANES)]
        m_row = slot_ref[0, :, pl.ds(off, V7X_LANES)]
        p_row = p_ref[0, :, pl.ds(off, V7X_LANES)]
        acc_i = acc_i + jnp.where(c_row <= slot, 1.0, 0.0)
        acc_g = acc_g + jnp.where(m_row == slot + 1.0, p_row, 0.0)
        return acc_i, acc_g

    zero = jnp.zeros((ROUTE_SLOTS, V7X_LANES), F32)
    acc_i, acc_g = lax.fori_loop(0, nt // V7X_LANES, chunk, (zero, zero))
    idx_ref[0] = jnp.sum(acc_i, axis=1, keepdims=True).astype(jnp.int32)
    gate_ref[0] = jnp.sum(acc_g, axis=1, keepdims=True)


def _route_call(probs_t, cap):
    ne, nt = probs_t.shape
    assert cap % ROUTE_SLOTS == 0 and nt % V7X_LANES == 0
    cnt, slot = pl.pallas_call(
        functools.partial(_route_kernel, cap=cap),
        out_shape=[jax.ShapeDtypeStruct((ne, nt), F32), jax.ShapeDtypeStruct((ne, nt), F32)],
        compiler_params=_cparams(None, VMEM_LIMIT_MID),
        name="route_select",
    )(probs_t)
    row = pl.BlockSpec((1, 1, nt), lambda e, s: (e, 0, 0))
    return pl.pallas_call(
        _slots_kernel,
        out_shape=[jax.ShapeDtypeStruct((ne, cap, 1), jnp.int32),
                   jax.ShapeDtypeStruct((ne, cap, 1), F32)],
        grid=(ne, cap // ROUTE_SLOTS),
        in_specs=[row, row, row],
        out_specs=[pl.BlockSpec((1, ROUTE_SLOTS, 1), lambda e, s: (e, s, 0)),
                   pl.BlockSpec((1, ROUTE_SLOTS, 1), lambda e, s: (e, s, 0))],
        compiler_params=_cparams(("parallel", "parallel"), VMEM_LIMIT_MID),
        name="route_slots",
    )(cnt.reshape(ne, 1, nt), slot.reshape(ne, 1, nt), probs_t.reshape(ne, 1, nt))


EXPERT_ROWS = 512
EXPERT_TF = 256
SCATTER_ROWS = 128


def _row_copy(src, si, dst, di, sem):
    return pltpu.make_async_copy(src.at[pl.ds(si, 1), :], dst.at[pl.ds(di, 1), :], sem)


def _expert_kernel(idx_ref, gate_ref, h_hbm, acc_in, wg_ref, wu_ref, wd_ref, acc_hbm,
                   xe, yacc, rbuf, sem_g, sem_s):
    del acc_in
    f = pl.program_id(2)
    nf = pl.num_programs(2)
    rows = xe.shape[0]

    @pl.when(f == 0)
    def _():
        def issue(j, carry):
            _row_copy(h_hbm, idx_ref[0, 0, j], yacc, j, sem_g).start()
            return carry
        lax.fori_loop(0, rows, issue, 0)
        pltpu.make_async_copy(h_hbm.at[pl.ds(0, rows), :], yacc, sem_g).wait()
        xe[...] = yacc[...].astype(BF16)

    x = xe[...]
    hg = jnp.dot(x, wg_ref[0].astype(BF16), preferred_element_type=F32)
    hu = jnp.dot(x, wu_ref[0].astype(BF16), preferred_element_type=F32)
    hh = (hg * _sigmoid(hg) * hu).astype(BF16)
    contrib = jnp.dot(hh, wd_ref[0].astype(BF16), preferred_element_type=F32)

    @pl.when(f == 0)
    def _():
        yacc[...] = contrib

    @pl.when(f > 0)
    def _():
        yacc[...] += contrib

    @pl.when(f == nf - 1)
    def _():
        nsub = rbuf.shape[0]
        for s0 in range(0, rows, nsub):
            def issue_r(j, carry):
                _row_copy(acc_hbm, idx_ref[0, 0, s0 + j], rbuf, j, sem_g).start()
                return carry
            lax.fori_loop(0, nsub, issue_r, 0)
            pltpu.make_async_copy(acc_hbm.at[pl.ds(0, nsub), :], rbuf, sem_g).wait()
            rbuf[...] = rbuf[...] + yacc[pl.ds(s0, nsub), :] * gate_ref[0, pl.ds(s0, nsub), :]

            def issue_w(j, carry):
                _row_copy(rbuf, j, acc_hbm, idx_ref[0, 0, s0 + j], sem_s).start()
                return carry
            lax.fori_loop(0, nsub, issue_w, 0)
            pltpu.make_async_copy(rbuf, acc_hbm.at[pl.ds(0, nsub), :], sem_s).wait()


def _expert_call(idx3, gate3, h2, acc, w_gate, w_up, w_down):
    nblk_total = idx3.shape[0]
    rows = idx3.shape[2]
    nblk = nblk_total // N_EXPERTS
    tf = EXPERT_TF
    return pl.pallas_call(
        _expert_kernel,
        out_shape=jax.ShapeDtypeStruct(acc.shape, F32),
        grid=(N_EXPERTS, nblk, D_FF // tf),
        in_specs=[pl.BlockSpec((1, 1, rows), lambda e, b, f: (e * nblk + b, 0, 0), memory_space=pltpu.SMEM),
                  pl.BlockSpec((1, rows, 1), lambda e, b, f: (e * nblk + b, 0, 0)),
                  pl.BlockSpec(memory_space=pl.ANY),
                  pl.BlockSpec(memory_space=pl.ANY),
                  pl.BlockSpec((1, D_MODEL, tf), lambda e, b, f: (e, 0, f)),
                  pl.BlockSpec((1, D_MODEL, tf), lambda e, b, f: (e, 0, f)),
                  pl.BlockSpec((1, tf, D_MODEL), lambda e, b, f: (e, f, 0))],
        out_specs=pl.BlockSpec(memory_space=pl.ANY),
        scratch_shapes=[pltpu.VMEM((rows, D_MODEL), BF16),
                        pltpu.VMEM((rows, D_MODEL), F32),
                        pltpu.VMEM((SCATTER_ROWS, D_MODEL), F32),
                        pltpu.SemaphoreType.DMA,
                        pltpu.SemaphoreType.DMA],
        input_output_aliases={3: 0},
        compiler_params=_cparams(("arbitrary", "arbitrary", "arbitrary"), VMEM_LIMIT_BIG),
        name="experts",
    )(idx3, gate3, h2, acc, w_gate, w_up, w_down)


def _final_kernel(x_ref, acc_ref, gate_ref, g_ref, o_ref):
    x = x_ref[...] + gate_ref[0] * acc_ref[...]
    ms = jnp.mean(x * x, axis=-1, keepdims=True)
    o_ref[...] = x * lax.rsqrt(ms + RMS_EPS) * g_ref[...]


def _final_call(x1, acc, modr, final_g, rows_per_mod, mod_base):
    tk = x1.shape[0]
    tm = 256
    return pl.pallas_call(
        _final_kernel,
        out_shape=jax.ShapeDtypeStruct((tk, D_MODEL), F32),
        grid=(tk // tm,),
        in_specs=[pl.BlockSpec((tm, D_MODEL), lambda i: (i, 0)),
                  pl.BlockSpec((tm, D_MODEL), lambda i: (i, 0)),
                  pl.BlockSpec((1, 1, D_MODEL), _mod_row_map(tm, rows_per_mod, mod_base, 5)),
                  pl.BlockSpec((1, D_MODEL), lambda i: (0, 0))],
        out_specs=pl.BlockSpec((tm, D_MODEL), lambda i: (i, 0)),
        compiler_params=_cparams(("parallel",), VMEM_LIMIT_MID),
        name="final_norm",
    )(x1, acc, modr, final_g)


def _layer(x3, modr, mod_base, per_batch_mod, s0, p, latent):
    nb, seq, _ = x3.shape
    tk = nb * seq
    rows_per_mod = seq if per_batch_mod else tk
    x = x3.reshape(tk, D_MODEL)

    z = _inproj_call(x, p['norm1_g'], modr, p['w_in_bf'], rows_per_mod, mod_base)
    zs = _conv_call(z.reshape(nb, seq, D_IN), p['conv_w'], latent).reshape(tk, C_SHIFT)
    lw, a, g = _prep_call(zs, p['w0'], p['w_lora_up'], p['a0'], p['a_up'], p['g_up'])
    y2, s_fin = _wkv_call(zs, lw, a, p['k_k'], p['k_a'], s0, nb, seq)
    ya = _rwkv_out_call(y2, zs, a, g, p['k_a'], p['r_k'], p['ln_x_g'], p['ln_x_b'])
    yb = _sgu_call(z, p['sgu_ln_g'], p['sgu_ln_b'], p['sgu_w'], p['sgu_b3'])
    x1 = _outproj_call(ya, yb, p['w_out_bf'], x, modr, rows_per_mod, mod_base)

    h2, probs_t = _norm2_call(x1, p['norm2_g'], modr, p['w_router_t'], rows_per_mod, mod_base)
    cap = CAPACITY_FACTOR * tk // N_EXPERTS
    idx, gate = _route_call(probs_t, cap)
    rows = min(EXPERT_ROWS, cap)
    idx3 = idx.reshape(N_EXPERTS * (cap // rows), 1, rows)
    gate3 = gate.reshape(N_EXPERTS * (cap // rows), rows, 1)
    acc = _expert_call(idx3, gate3, h2, jnp.zeros((tk, D_MODEL), F32),
                       p['w_exp_gate'], p['w_exp_up'], p['w_exp_down'])
    out = _final_call(x1, acc, modr, p['final_g'], rows_per_mod, mod_base)
    return out.reshape(nb, seq, D_MODEL), s_fin


def _block_diag_state(s):
    nb = s.shape[0]
    s6 = s.reshape(nb, 2, N_PAIRS_TOTAL, 2, HEAD_A, HEAD_A)
    eye = jnp.eye(2, dtype=s.dtype)
    bd = jnp.einsum('bdpqvk,qr->bdpqvrk', s6, eye)
    return bd.reshape(nb, 2, N_PAIRS_TOTAL, PAIR, PAIR)


def _head_states(s_bd):
    nb = s_bd.shape[0]
    s7 = s_bd.reshape(nb, 2, N_PAIRS_TOTAL, 2, HEAD_A, 2, HEAD_A)
    blocks = jnp.stack([s7[:, :, :, 0, :, 0, :], s7[:, :, :, 1, :, 1, :]], axis=3)
    return blocks.reshape(nb, 2, H_A, HEAD_A, HEAD_A)


def kernel(x_prompt, x_sample, state_wkv, c, c_ctx, norm1_g, norm2_g, w_mod, b_mod, w_in, conv_w, w0, w_lora_up, a0, a_up, g_up, k_k, k_a, r_k, ln_x_g, ln_x_b, sgu_ln_g, sgu_ln_b, sgu_w, sgu_b, w_out, w_router, w_exp_gate, w_exp_up, w_exp_down, final_g):
    depth = w_in.shape[0]
    nb_c = x_prompt.shape[0]
    nb_l = x_sample.shape[0]
    mod_rows = 8
    cvec = jnp.concatenate([c_ctx[None, :], c, jnp.zeros((mod_rows - 1 - nb_l, D_MODEL), F32)], axis=0)

    xc, xl = x_prompt, x_sample
    new_states = []
    for l in range(depth):
        p = {
            'norm1_g': norm1_g[l][None, :], 'norm2_g': norm2_g[l][None, :],
            'w_in_bf': w_in[l].astype(BF16), 'conv_w': conv_w[l],
            'w0': w0[l], 'w_lora_up': w_lora_up[l].reshape(2 * LORA_W, D_A),
            'a0': a0[l], 'a_up': a_up[l].reshape(2 * LORA_A, D_A), 'g_up': g_up[l],
            'k_k': k_k[l][None, :], 'k_a': k_a[l][None, :], 'r_k': r_k[l][None, :],
            'ln_x_g': ln_x_g[l][None, :], 'ln_x_b': ln_x_b[l][None, :],
            'sgu_ln_g': sgu_ln_g[l][None, :], 'sgu_ln_b': sgu_ln_b[l][None, :],
            'sgu_w': sgu_w[l], 'sgu_b3': sgu_b[l][:, :, None],
            'w_out_bf': w_out[l].astype(BF16), 'w_router_t': w_router[l].T,
            'w_exp_gate': w_exp_gate[l], 'w_exp_up': w_exp_up[l], 'w_exp_down': w_exp_down[l],
            'final_g': final_g[None, :],
        }
        mod = _mod_call(cvec, w_mod[l], b_mod[l][None, :])
        modr = mod.reshape(mod_rows * N_MOD, 1, D_MODEL)

        s0_c = jnp.zeros((nb_c, 2, N_PAIRS_TOTAL, PAIR, PAIR), F32)
        xc, s_fin = _layer(xc, modr, 0, False, s0_c, p, latent=False)
        new_states.append(_head_states(s_fin).astype(x_prompt.dtype))

        s0_l = _block_diag_state(state_wkv[:, l].astype(F32))
        xl, _ = _layer(xl, modr, 1, True, s0_l, p, latent=True)

    assert depth == 1
    state_new = jnp.stack(new_states, axis=1)
    return (xc, xl, state_new)
```

```python
import functools

import jax
import jax.numpy as jnp
from jax import lax
from jax.experimental import pallas as pl
from jax.experimental.pallas import tpu as pltpu

F32 = jnp.float32
BF16 = jnp.bfloat16

D_MODEL = 4096
D_A = 2048
HEAD_A = 64
H_A = 32
D_B = 2048
G_B = 8
CH_B = 256
SGU_CHUNK = 128
LORA_W = 64
LORA_A = 64
LORA_G = 256
C_SHIFT = 3 * D_A + 2 * LORA_W + 2 * LORA_A + LORA_G
D_IN = C_SHIFT + 2 * D_B
N_EXPERTS = 16
CAPACITY_FACTOR = 2
D_FF = 5632
N_MOD = 6
GRID_W = 64
RMS_EPS = 1e-6
GN_EPS = 64e-5
LN_EPS = 1e-5

V7X_LANES = 128
V7X_SUBLANES = 8
V7X_VMEM_BYTES = 64 * 1024 * 1024
VMEM_LIMIT_BIG = 52 * 1024 * 1024
VMEM_LIMIT_MID = 40 * 1024 * 1024

WKV_CHUNK = 64
PAIR = 2 * HEAD_A
WKV_GW = 2048
N_PAIRS_TOTAL = D_A // PAIR


def _cparams(sem, vmem=None):
    return pltpu.CompilerParams(dimension_semantics=sem, vmem_limit_bytes=vmem)


def _bdot(a, b):
    return jnp.dot(a.astype(BF16), b.astype(BF16), preferred_element_type=F32)


def _nt(a, b):
    return lax.dot_general(a.astype(BF16), b.astype(BF16), (((1,), (1,)), ((), ())),
                           preferred_element_type=F32)


def _tn(a, b):
    return lax.dot_general(a.astype(BF16), b.astype(BF16), (((0,), (0,)), ((), ())),
                           preferred_element_type=F32)


def _split2(x):
    hi = x.astype(BF16)
    lo = (x - hi.astype(F32)).astype(BF16)
    return hi, lo


def _split3(x):
    hi = x.astype(BF16)
    r1 = x - hi.astype(F32)
    mid = r1.astype(BF16)
    lo = (r1 - mid.astype(F32)).astype(BF16)
    return hi, mid, lo


def _dot3(a, b):
    ah, al = _split2(a)
    bh, bl = _split2(b)
    d = functools.partial(jnp.dot, preferred_element_type=F32)
    return d(ah, bh) + (d(al, bh) + d(ah, bl))


def _dot_exact_lhs(a_bf, b):
    bh, bm, bl = _split3(b)
    d = functools.partial(jnp.dot, preferred_element_type=F32)
    return d(a_bf, bh) + (d(a_bf, bm) + d(a_bf, bl))


def _dot_exact_rhs(a, b_bf):
    ah, am, al = _split3(a)
    d = functools.partial(jnp.dot, preferred_element_type=F32)
    return d(ah, b_bf) + (d(am, b_bf) + d(al, b_bf))


def _sigmoid(x):
    return jax.nn.sigmoid(x)


def _mod_kernel(c_ref, w_ref, b_ref, o_ref):
    cv = c_ref[...]
    s = cv * _sigmoid(cv)
    o_ref[...] = _dot3(s, w_ref[...]) + b_ref[...]


def _mod_call(cvec, w_mod, b_mod):
    rows, d = cvec.shape
    n = w_mod.shape[1]
    tn = 512
    return pl.pallas_call(
        _mod_kernel,
        out_shape=jax.ShapeDtypeStruct((rows, n), F32),
        grid=(n // tn,),
        in_specs=[pl.BlockSpec((rows, d), lambda j: (0, 0)),
                  pl.BlockSpec((d, tn), lambda j: (0, j)),
                  pl.BlockSpec((1, tn), lambda j: (0, j))],
        out_specs=pl.BlockSpec((rows, tn), lambda j: (0, j)),
        compiler_params=_cparams(("parallel",), VMEM_LIMIT_MID),
        name="mod_proj",
    )(cvec, w_mod, b_mod)


def _inproj_kernel(x_ref, g_ref, sc_ref, sh_ref, w_ref, o_ref, h_scr):
    @pl.when(pl.program_id(1) == 0)
    def _():
        x = x_ref[...]
        ms = jnp.mean(x * x, axis=-1, keepdims=True)
        y = x * lax.rsqrt(ms + RMS_EPS) * g_ref[...]
        h_scr[...] = (y * (1.0 + sc_ref[0]) + sh_ref[0]).astype(BF16)

    o_ref[...] = jnp.dot(h_scr[...], w_ref[...], preferred_element_type=F32)


def _mod_row_map(tm, rows_per_mod, mod_base, comp):
    def index_map(i, *_):
        return ((mod_base + (i * tm) // rows_per_mod) * N_MOD + comp, 0, 0)
    return index_map


def _inproj_call(x, norm_g, modr, w_bf, rows_per_mod, mod_base):
    tk, d = x.shape
    n = w_bf.shape[1]
    tm, tn = 512, 768
    return pl.pallas_call(
        _inproj_kernel,
        out_shape=jax.ShapeDtypeStruct((tk, n), F32),
        grid=(tk // tm, n // tn),
        in_specs=[pl.BlockSpec((tm, d), lambda i, j: (i, 0)),
                  pl.BlockSpec((1, d), lambda i, j: (0, 0)),
                  pl.BlockSpec((1, 1, d), _mod_row_map(tm, rows_per_mod, mod_base, 1)),
                  pl.BlockSpec((1, 1, d), _mod_row_map(tm, rows_per_mod, mod_base, 0)),
                  pl.BlockSpec((d, tn), lambda i, j: (0, j))],
        out_specs=pl.BlockSpec((tm, tn), lambda i, j: (i, j)),
        scratch_shapes=[pltpu.VMEM((tm, d), BF16)],
        compiler_params=_cparams(("parallel", "arbitrary"), VMEM_LIMIT_BIG),
        name="in_proj",
    )(x, norm_g, modr, modr, w_bf)


CONV_PAD = 72
CONV_ROWS = 256


def _conv_kernel(z_ref, w_ref, o_ref, ext, *, seq, width, vertical):
    bb = z_ref.shape[0]
    tc = z_ref.shape[2]
    zeros = jnp.zeros((CONV_PAD, tc), F32)
    ext[pl.ds(0, CONV_PAD), :] = zeros
    ext[pl.ds(CONV_PAD + seq, CONV_PAD), :] = zeros
    rows = min(CONV_ROWS, seq)
    col = lax.broadcasted_iota(jnp.int32, (rows, tc), 0)
    for b in range(bb):
        ext[pl.ds(CONV_PAD, seq), :] = z_ref[b]
        for r0 in range(0, seq, rows):
            cpos = (col + r0) % width
            acc = jnp.zeros((rows, tc), F32)
            for di in ((-1, 0, 1) if vertical else (0,)):
                for dj in (-1, 0, 1):
                    tap = ext[pl.ds(CONV_PAD + r0 + di * width + dj, rows), :]
                    if dj == -1:
                        tap = jnp.where(cpos == 0, 0.0, tap)
                    elif dj == 1:
                        tap = jnp.where(cpos == width - 1, 0.0, tap)
                    acc = acc + tap * w_ref[di + 1, pl.ds(dj + 1, 1), :]
            o_ref[b, pl.ds(r0, rows), :] = acc


def _conv_call(z3, conv_w, latent):
    nb, seq, _ = z3.shape
    if latent:
        bb, tc, width = 1, 256, GRID_W
    else:
        bb, tc, width = 4, 512, seq
    kern = functools.partial(_conv_kernel, seq=seq, width=width, vertical=latent)
    return pl.pallas_call(
        kern,
        out_shape=jax.ShapeDtypeStruct((nb, seq, C_SHIFT), F32),
        grid=(nb // bb, C_SHIFT // tc),
        in_specs=[pl.BlockSpec((bb, seq, tc), lambda b, j: (b, 0, j)),
                  pl.BlockSpec((3, 3, tc), lambda b, j: (0, 0, j))],
        out_specs=pl.BlockSpec((bb, seq, tc), lambda b, j: (b, 0, j)),
        scratch_shapes=[pltpu.VMEM((seq + 2 * CONV_PAD, tc), F32)],
        compiler_params=_cparams(("parallel", "parallel"), VMEM_LIMIT_MID),
        name="short_conv",
    )(z3, conv_w)


def _prep_kernel(zl_ref, w0_ref, wup_ref, a0_ref, aup_ref, gup_ref, lw_ref, a_ref, g_ref):
    zl = zl_ref[...]
    tm = zl.shape[0]
    wd = jnp.tanh(zl[:, 0:2 * LORA_W])
    ad = zl[:, 2 * LORA_W:2 * LORA_W + 2 * LORA_A]
    gd = _sigmoid(zl[:, 2 * LORA_W + 2 * LORA_A:])
    lane = lax.broadcasted_iota(jnp.int32, (tm, 2 * LORA_W), 1)
    for d in range(2):
        sel = (lane < LORA_W) if d == 0 else (lane >= LORA_W)
        wlin = w0_ref[pl.ds(d, 1), :] + _dot3(jnp.where(sel, wd, 0.0), wup_ref[...])
        w_log = -jax.nn.softplus(-wlin) - 0.5
        lw_ref[d] = -jnp.exp(w_log)
        alin = a0_ref[pl.ds(d, 1), :] + _dot3(jnp.where(sel, ad, 0.0), aup_ref[...])
        a_ref[d] = _sigmoid(alin)
    g_ref[...] = _dot3(gd, gup_ref[...])


def _prep_call(zs, w0, wup, a0, aup, gup):
    tk = zs.shape[0]
    tm = 256
    lblk = (3 * D_A) // 512
    return pl.pallas_call(
        _prep_kernel,
        out_shape=[jax.ShapeDtypeStruct((2, tk, D_A), F32),
                   jax.ShapeDtypeStruct((2, tk, D_A), F32),
                   jax.ShapeDtypeStruct((tk, D_A), F32)],
        grid=(tk // tm,),
        in_specs=[pl.BlockSpec((tm, 512), lambda i: (i, lblk)),
                  pl.BlockSpec((2, D_A), lambda i: (0, 0)),
                  pl.BlockSpec((2 * LORA_W, D_A), lambda i: (0, 0)),
                  pl.BlockSpec((2, D_A), lambda i: (0, 0)),
                  pl.BlockSpec((2 * LORA_A, D_A), lambda i: (0, 0)),
                  pl.BlockSpec((LORA_G, D_A), lambda i: (0, 0))],
        out_specs=[pl.BlockSpec((2, tm, D_A), lambda i: (0, i, 0)),
                   pl.BlockSpec((2, tm, D_A), lambda i: (0, i, 0)),
                   pl.BlockSpec((tm, D_A), lambda i: (i, 0))],
        compiler_params=_cparams(("parallel",), VMEM_LIMIT_MID),
        name="rwkv_prep",
    )(zs, w0, wup, a0, aup, gup)


def _wkv_kernel(r_ref, k_ref, v_ref, lw_ref, a_ref, kk_ref, ka_ref, s0_ref, y_ref, sout_ref, s_scr):
    d = pl.program_id(1)
    ci = pl.program_id(3)
    nci = pl.num_programs(3)
    C = r_ref.shape[0]
    gw = r_ref.shape[1]
    n_pairs = gw // PAIR

    @pl.when(ci == 0)
    def _():
        s_scr[...] = s0_ref[0, 0]

    rev = d == 1
    ri = lax.broadcasted_iota(jnp.int32, (C, C), 0)
    cj = lax.broadcasted_iota(jnp.int32, (C, C), 1)
    before = jnp.where(rev, cj - ri, ri - cj) >= 0
    tri = jnp.where(before, 1.0, 0.0).astype(BF16)

    lw = lw_ref[0]
    cum = _dot_exact_lhs(tri, lw)
    total = jnp.where(rev, cum[0:1, :], cum[C - 1:C, :])

    r2 = lax.broadcasted_iota(jnp.int32, (2 * C, 2 * C), 0)
    c2 = lax.broadcasted_iota(jnp.int32, (2 * C, 2 * C), 1)
    same = (r2 // C) == (c2 // C)
    dd = jnp.where(rev, (c2 % C) - (r2 % C), (r2 % C) - (c2 % C))
    strict = same & (dd > 0)
    incl = same & (dd >= 0)
    eye = jnp.where(r2 == c2, 1.0, 0.0)
    merge_masks = []
    s = 1
    while s < C:
        merge_masks.append(((r2 // (2 * s)) == (c2 // (2 * s))) & ((r2 // s) != (c2 // s)))
        s *= 2
    lane = lax.broadcasted_iota(jnp.int32, (C, PAIR), 1)
    first = lane < HEAD_A
    hr = lax.broadcasted_iota(jnp.int32, (PAIR, PAIR), 0) // HEAD_A
    hc = lax.broadcasted_iota(jnp.int32, (PAIR, PAIR), 1) // HEAD_A
    head_ones = jnp.where(hr == hc, 1.0, 0.0).astype(BF16)

    def pairs(x):
        return jnp.stack([x[:, p * PAIR:(p + 1) * PAIR] for p in range(n_pairs)], axis=0)

    def stack(x):
        return jnp.concatenate([jnp.where(first, x, 0.0), jnp.where(first, 0.0, x)], axis=1)

    def bnn(x, y):
        return lax.dot_general(x.astype(BF16), y.astype(BF16), (((2,), (1,)), ((0,), (0,))),
                               preferred_element_type=F32)

    def bnt(x, y):
        return lax.dot_general(x.astype(BF16), y.astype(BF16), (((2,), (2,)), ((0,), (0,))),
                               preferred_element_type=F32)

    r = pairs(r_ref[...])
    k = pairs(k_ref[...])
    v = pairs(v_ref[...])
    a = pairs(a_ref[0])
    lwp = pairs(lw)
    cp = pairs(cum)
    tot = pairs(total)
    kkr = k * pairs(kk_ref[...])
    n2 = _dot_exact_rhs((kkr * kkr).reshape(n_pairs * C, PAIR), head_ones).reshape(n_pairs, C, PAIR)
    kk = kkr / jnp.maximum(jnp.sqrt(n2), 1e-12)
    kd = k * (1.0 + (a - 1.0) * pairs(ka_ref[...]))
    b = kk * a
    e_in = jnp.exp(cp)
    e_ex = jnp.exp(cp - lwp)
    e_ng = jnp.exp(-cp)
    e_tot = jnp.exp(tot - cp)
    s_a = stack(kk * e_ex)
    s_r = stack(r * e_in)
    s_b = stack(b * e_ng)
    s_k = stack(kd * e_ng)
    s_kp = stack(kd * e_tot)
    s_bp = stack(b * e_tot)
    s_v = stack(v)

    g = bnt(jnp.concatenate([s_a, s_r], axis=1), jnp.concatenate([s_b, s_k], axis=1))
    n = jnp.where(strict, -g[:, 0:2 * C, 0:2 * C], 0.0)
    m_ak = jnp.where(strict, g[:, 0:2 * C, 2 * C:], 0.0)
    m_rb = jnp.where(incl, g[:, 2 * C:, 0:2 * C], 0.0)
    m_rk = jnp.where(incl, g[:, 2 * C:, 2 * C:], 0.0)

    t = eye + jnp.where(merge_masks[0], n, 0.0)
    for mask in merge_masks[1:]:
        t = t + bnn(t, bnn(jnp.where(mask, n, 0.0), t))

    w = bnn(jnp.concatenate([m_ak, m_rk], axis=1), s_v)
    ta = bnn(t, jnp.concatenate([s_a, w[:, 0:2 * C]], axis=2))
    s_at = ta[:, :, 0:PAIR]
    u_hat = ta[:, :, PAIR:]

    s_old = s_scr[...]
    z = bnt(jnp.concatenate([s_at, s_r], axis=1), s_old)
    u = z[:, 0:2 * C] + u_hat
    y_st = z[:, 2 * C:] + w[:, 2 * C:] - bnn(m_rb, u)
    y = y_st[:, 0:C] + y_st[:, C:]
    decay = jnp.exp(tot)
    for p in range(n_pairs):
        y_ref[0, :, p * PAIR:(p + 1) * PAIR] = y[p]
        s_scr[p] = s_old[p] * decay[p] + _tn(jnp.concatenate([s_v[p], u[p]], axis=0),
                                             jnp.concatenate([s_kp[p], -s_bp[p]], axis=0))

    @pl.when(ci == nci - 1)
    def _():
        for p in range(n_pairs):
            s_pair = s_scr[p]
            sout_ref[0, 0, 2 * p] = s_pair[0:HEAD_A, 0:HEAD_A]
            sout_ref[0, 0, 2 * p + 1] = s_pair[HEAD_A:, HEAD_A:]


def _wkv_call(zs, lw, a, k_k, k_a, s0_bd, nb, seq):
    tk = zs.shape[0]
    C = WKV_CHUNK
    nch = seq // C
    gw = WKV_GW
    ng = D_A // gw
    npair = gw // PAIR
    koff = D_A // gw

    def tokmap(b, d, g, c):
        return b * nch + c + d * (nch - 1 - 2 * c)

    return pl.pallas_call(
        _wkv_kernel,
        out_shape=[jax.ShapeDtypeStruct((2, tk, D_A), F32),
                   jax.ShapeDtypeStruct((nb, 2, H_A, HEAD_A, HEAD_A), F32)],
        grid=(nb, 2, ng, nch),
        in_specs=[pl.BlockSpec((C, gw), lambda b, d, g, c: (tokmap(b, d, g, c), g)),
                  pl.BlockSpec((C, gw), lambda b, d, g, c: (tokmap(b, d, g, c), koff + g)),
                  pl.BlockSpec((C, gw), lambda b, d, g, c: (tokmap(b, d, g, c), 2 * koff + g)),
                  pl.BlockSpec((1, C, gw), lambda b, d, g, c: (d, tokmap(b, d, g, c), g)),
                  pl.BlockSpec((1, C, gw), lambda b, d, g, c: (d, tokmap(b, d, g, c), g)),
                  pl.BlockSpec((1, gw), lambda b, d, g, c: (0, g)),
                  pl.BlockSpec((1, gw), lambda b, d, g, c: (0, g)),
                  pl.BlockSpec((1, 1, npair, PAIR, PAIR), lambda b, d, g, c: (b, d, g, 0, 0))],
        out_specs=[pl.BlockSpec((1, C, gw), lambda b, d, g, c: (d, tokmap(b, d, g, c), g)),
                   pl.BlockSpec((1, 1, 2 * npair, HEAD_A, HEAD_A), lambda b, d, g, c: (b, d, g, 0, 0))],
        scratch_shapes=[pltpu.VMEM((npair, PAIR, PAIR), F32)],
        compiler_params=_cparams(("parallel", "parallel", "parallel", "arbitrary"), VMEM_LIMIT_MID),
        name="wkv_scan",
    )(zs, zs, zs, lw, a, k_k, k_a, s0_bd)


def _rwkv_out_kernel(y_ref, r_ref, k_ref, v_ref, a_ref, g_ref, ka_ref, rk_ref, lg_ref, lb_ref, o_ref):
    gw = r_ref.shape[1]
    hr = lax.broadcasted_iota(jnp.int32, (PAIR, PAIR), 0) // HEAD_A
    hc = lax.broadcasted_iota(jnp.int32, (PAIR, PAIR), 1) // HEAD_A
    head_ones = jnp.where(hr == hc, 1.0, 0.0).astype(BF16)
    for p in range(gw // PAIR):
        sl = slice(p * PAIR, (p + 1) * PAIR)
        y = y_ref[0, :, sl] + y_ref[1, :, sl]
        mean = _dot_exact_rhs(y, head_ones) * (1.0 / HEAD_A)
        yc = y - mean
        var = _dot_exact_rhs(yc * yc, head_ones) * (1.0 / HEAD_A)
        yn = yc * lax.rsqrt(var + GN_EPS) * lg_ref[:, sl] + lb_ref[:, sl]
        r = r_ref[:, sl]
        k = k_ref[:, sl]
        ka = ka_ref[:, sl]
        kd0 = k * (1.0 + (a_ref[0, :, sl] - 1.0) * ka)
        kd1 = k * (1.0 + (a_ref[1, :, sl] - 1.0) * ka)
        dotp = _dot_exact_rhs(r * ((kd0 + kd1) * rk_ref[:, sl]), head_ones)
        o_ref[:, sl] = ((yn + dotp * v_ref[:, sl]) * g_ref[:, sl]).astype(BF16)


def _rwkv_out_call(y2, zs, a, g, k_a, r_k, ln_g, ln_b):
    tk = zs.shape[0]
    tm, gw = 256, 512
    koff = D_A // gw
    vec = pl.BlockSpec((1, gw), lambda i, j: (0, j))
    return pl.pallas_call(
        _rwkv_out_kernel,
        out_shape=jax.ShapeDtypeStruct((tk, D_A), BF16),
        grid=(tk // tm, D_A // gw),
        in_specs=[pl.BlockSpec((2, tm, gw), lambda i, j: (0, i, j)),
                  pl.BlockSpec((tm, gw), lambda i, j: (i, j)),
                  pl.BlockSpec((tm, gw), lambda i, j: (i, koff + j)),
                  pl.BlockSpec((tm, gw), lambda i, j: (i, 2 * koff + j)),
                  pl.BlockSpec((2, tm, gw), lambda i, j: (0, i, j)),
                  pl.BlockSpec((tm, gw), lambda i, j: (i, j)),
                  vec, vec, vec, vec],
        out_specs=pl.BlockSpec((tm, gw), lambda i, j: (i, j)),
        compiler_params=_cparams(("parallel", "parallel"), VMEM_LIMIT_MID),
        name="rwkv_out",
    )(y2, zs, zs, zs, a, g, k_a, r_k, ln_g, ln_b)


def _gelu(x):
    return 0.5 * x * (1.0 + lax.erf(x * 0.7071067811865476))


def _sgu_kernel(*refs):
    nblk = (len(refs) - 5) // 2
    zu_refs, zv_refs = refs[:nblk], refs[nblk:2 * nblk]
    lg_ref, lb_ref, ws_ref, bs_ref, o_ref = refs[2 * nblk:]
    tm = o_ref.shape[0]
    per_blk = zu_refs[0].shape[1] // CH_B
    for grp in range(G_B):
        zu_ref, zv_ref = zu_refs[grp // per_blk], zv_refs[grp // per_blk]
        bsl = slice((grp % per_blk) * CH_B, (grp % per_blk + 1) * CH_B)
        sl = slice(grp * CH_B, (grp + 1) * CH_B)
        vf = _gelu(zv_ref[:, bsl])
        mean = jnp.mean(vf, axis=-1, keepdims=True)
        vc = vf - mean
        var = jnp.mean(vc * vc, axis=-1, keepdims=True)
        vn = vc * lax.rsqrt(var + LN_EPS) * lg_ref[:, sl] + lb_ref[:, sl]
        for c0 in range(0, tm, SGU_CHUNK):
            mixed = _bdot(ws_ref[grp], vn[c0:c0 + SGU_CHUNK]) + bs_ref[grp]
            u = _gelu(zu_ref[pl.ds(c0, SGU_CHUNK), bsl])
            o_ref[pl.ds(c0, SGU_CHUNK), sl] = (u * mixed).astype(BF16)


def _sgu_call(z, ln_g, ln_b, w_s, b_s3):
    tk = z.shape[0]
    tm = 256
    bw = 512
    u0 = C_SHIFT // bw
    v0 = (C_SHIFT + D_B) // bw
    nbw = D_B // bw
    in_specs = ([pl.BlockSpec((tm, bw), functools.partial(lambda i, q: (i, q), q=u0 + q)) for q in range(nbw)]
                + [pl.BlockSpec((tm, bw), functools.partial(lambda i, q: (i, q), q=v0 + q)) for q in range(nbw)]
                + [pl.BlockSpec((1, D_B), lambda i: (0, 0)),
                   pl.BlockSpec((1, D_B), lambda i: (0, 0)),
                   pl.BlockSpec((G_B, SGU_CHUNK, SGU_CHUNK), lambda i: (0, 0, 0)),
                   pl.BlockSpec((G_B, SGU_CHUNK, 1), lambda i: (0, 0, 0))])
    return pl.pallas_call(
        _sgu_kernel,
        out_shape=jax.ShapeDtypeStruct((tk, D_B), BF16),
        grid=(tk // tm,),
        in_specs=in_specs,
        out_specs=pl.BlockSpec((tm, D_B), lambda i: (i, 0)),
        compiler_params=_cparams(("parallel",), VMEM_LIMIT_MID),
        name="sgu",
    )(*([z] * (2 * nbw)), ln_g, ln_b, w_s, b_s3)


def _outproj_kernel(ya_ref, yb_ref, wa_ref, wb_ref, x_ref, gate_ref, o_ref):
    acc = jnp.dot(ya_ref[...], wa_ref[...], preferred_element_type=F32)
    acc = acc + jnp.dot(yb_ref[...], wb_ref[...], preferred_element_type=F32)
    o_ref[...] = x_ref[...] + gate_ref[0] * acc


def _outproj_call(ya, yb, w_bf, x, modr, rows_per_mod, mod_base):
    tk = x.shape[0]
    tm, tn = 512, 1024

    def gate_map(i, j):
        return ((mod_base + (i * tm) // rows_per_mod) * N_MOD + 2, 0, j)

    return pl.pallas_call(
        _outproj_kernel,
        out_shape=jax.ShapeDtypeStruct((tk, D_MODEL), F32),
        grid=(tk // tm, D_MODEL // tn),
        in_specs=[pl.BlockSpec((tm, D_A), lambda i, j: (i, 0)),
                  pl.BlockSpec((tm, D_B), lambda i, j: (i, 0)),
                  pl.BlockSpec((D_A, tn), lambda i, j: (0, j)),
                  pl.BlockSpec((D_B, tn), lambda i, j: (1, j)),
                  pl.BlockSpec((tm, tn), lambda i, j: (i, j)),
                  pl.BlockSpec((1, 1, tn), gate_map)],
        out_specs=pl.BlockSpec((tm, tn), lambda i, j: (i, j)),
        compiler_params=_cparams(("parallel", "arbitrary"), VMEM_LIMIT_MID),
        name="out_proj",
    )(ya, yb, w_bf, w_bf, x, modr)


def _norm2_kernel(x_ref, g_ref, sc_ref, sh_ref, wr_ref, *rest):
    h_ref, p_ref = rest[-2:]
    x = x_ref[...]
    ms = jnp.mean(x * x, axis=-1, keepdims=True)
    y = x * lax.rsqrt(ms + RMS_EPS) * g_ref[...]
    h = y * (1.0 + sc_ref[0]) + sh_ref[0]
    h_ref[...] = h
    hh, hl = _split2(h)
    wh, wl = _split2(wr_ref[...])
    nt = functools.partial(lax.dot_general, dimension_numbers=(((1,), (1,)), ((), ())),
                           preferred_element_type=F32)
    logits = nt(wh, hh) + (nt(wh, hl) + nt(wl, hh))
    m = jnp.max(logits, axis=0, keepdims=True)
    e = jnp.exp(logits - m)
    p_ref[...] = e / jnp.sum(e, axis=0, keepdims=True)


def _norm2_call(x1, norm_g, modr, w_router_t, rows_per_mod, mod_base, h_all, row_off, total_rows):
    tk = x1.shape[0]
    tm = 256
    blk_off = row_off // tm
    in_specs = [pl.BlockSpec((tm, D_MODEL), lambda i: (i, 0)),
                pl.BlockSpec((1, D_MODEL), lambda i: (0, 0)),
                pl.BlockSpec((1, 1, D_MODEL), _mod_row_map(tm, rows_per_mod, mod_base, 4)),
                pl.BlockSpec((1, 1, D_MODEL), _mod_row_map(tm, rows_per_mod, mod_base, 3)),
                pl.BlockSpec((N_EXPERTS, D_MODEL), lambda i: (0, 0))]
    args = [x1, norm_g, modr, modr, w_router_t]
    aliases = {}
    if h_all is not None:
        in_specs.append(pl.BlockSpec(memory_space=pl.ANY))
        args.append(h_all)
        aliases = {5: 0}
    return pl.pallas_call(
        _norm2_kernel,
        out_shape=[jax.ShapeDtypeStruct((total_rows, D_MODEL), F32),
                   jax.ShapeDtypeStruct((N_EXPERTS, tk), F32)],
        grid=(tk // tm,),
        in_specs=in_specs,
        out_specs=[pl.BlockSpec((tm, D_MODEL), lambda i: (i + blk_off, 0)),
                   pl.BlockSpec((N_EXPERTS, tm), lambda i: (0, i))],
        input_output_aliases=aliases,
        compiler_params=_cparams(("parallel",), VMEM_LIMIT_MID),
        name="norm2_router",
    )(*args)


ROUTE_SLOTS = 128


def _route_kernel(p_ref, cnt_ref, slot_ref, *, cap):
    ne, nt = p_ref.shape
    cnt_scr, sel_scr = cnt_ref, slot_ref
    p = p_ref[...]
    bits = pltpu.bitcast(p, jnp.int32)

    def count_ge(v):
        return jnp.sum(jnp.where(bits >= v, 1.0, 0.0), axis=1, keepdims=True)

    def bis(it, cur):
        cand = cur | (jnp.int32(1) << (30 - it))
        return jnp.where(count_ge(cand) >= cap, cand, cur)

    thr = lax.fori_loop(0, 31, bis, jnp.zeros((ne, 1), jnp.int32))
    gt = bits > thr
    eq = bits == thr
    need = cap - jnp.sum(jnp.where(gt, 1.0, 0.0), axis=1, keepdims=True)

    lr = lax.broadcasted_iota(jnp.int32, (V7X_LANES, V7X_LANES), 0)
    lc = lax.broadcasted_iota(jnp.int32, (V7X_LANES, V7X_LANES), 1)
    upper = jnp.where(lr <= lc, 1.0, 0.0).astype(BF16)

    def prefix(mask_f32, dst):
        carry = jnp.zeros((ne, 1), F32)
        for c0 in range(0, nt, V7X_LANES):
            blk = jnp.dot(mask_f32[:, c0:c0 + V7X_LANES].astype(BF16), upper,
                          preferred_element_type=F32) + carry
            dst[:, c0:c0 + V7X_LANES] = blk
            carry = blk[:, V7X_LANES - 1:V7X_LANES]

    prefix(jnp.where(eq, 1.0, 0.0), cnt_scr)
    sel = gt | (eq & (cnt_scr[...] <= need))
    prefix(jnp.where(sel, 1.0, 0.0), cnt_scr)
    sel_scr[...] = jnp.where(sel, cnt_scr[...], -1.0)


def _slots_kernel(cnt_ref, slot_ref, p_ref, idx_ref, gate_ref):
    nt = cnt_ref.shape[2]
    sb = pl.program_id(1)
    slot = (lax.broadcasted_iota(jnp.int32, (ROUTE_SLOTS, V7X_LANES), 0) + sb * ROUTE_SLOTS).astype(F32)

    def chunk(c, accs):
        acc_i, acc_g = accs
        off = pl.multiple_of(c * V7X_LANES, V7X_LANES)
        c_row = cnt_ref[0, :, pl.ds(off, V7X_LANES)]
        m_row = slot_ref[0, :, pl.ds(off, V7X_LANES)]
        p_row = p_ref[0, :, pl.ds(off, V7X_LANES)]
        acc_i = acc_i + jnp.where(c_row <= slot, 1.0, 0.0)
        acc_g = acc_g + jnp.where(m_row == slot + 1.0, p_row, 0.0)
        return acc_i, acc_g

    zero = jnp.zeros((ROUTE_SLOTS, V7X_LANES), F32)
    acc_i, acc_g = lax.fori_loop(0, nt // V7X_LANES, chunk, (zero, zero))
    idx_ref[0] = jnp.sum(acc_i, axis=1, keepdims=True).astype(jnp.int32)
    gate_ref[0] = jnp.sum(acc_g, axis=1, keepdims=True)


def _route_call(probs_t, cap):
    ne, nt = probs_t.shape
    assert cap % ROUTE_SLOTS == 0 and nt % V7X_LANES == 0
    cnt, slot = pl.pallas_call(
        functools.partial(_route_kernel, cap=cap),
        out_shape=[jax.ShapeDtypeStruct((ne, nt), F32), jax.ShapeDtypeStruct((ne, nt), F32)],
        compiler_params=_cparams(None, VMEM_LIMIT_MID),
        name="route_select",
    )(probs_t)
    row = pl.BlockSpec((1, 1, nt), lambda e, s: (e, 0, 0))
    return pl.pallas_call(
        _slots_kernel,
        out_shape=[jax.ShapeDtypeStruct((ne, cap, 1), jnp.int32),
                   jax.ShapeDtypeStruct((ne, cap, 1), F32)],
        grid=(ne, cap // ROUTE_SLOTS),
        in_specs=[row, row, row],
        out_specs=[pl.BlockSpec((1, ROUTE_SLOTS, 1), lambda e, s: (e, s, 0)),
                   pl.BlockSpec((1, ROUTE_SLOTS, 1), lambda e, s: (e, s, 0))],
        compiler_params=_cparams(("parallel", "parallel"), VMEM_LIMIT_MID),
        name="route_slots",
    )(cnt.reshape(ne, 1, nt), slot.reshape(ne, 1, nt), probs_t.reshape(ne, 1, nt))


EXPERT_TF = 256
EXPERT_TN = 256
GATHER_ROWS = 256
SCATTER_ROWS = 256


def _row_copy(src, si, dst, di, sem):
    return pltpu.make_async_copy(src.at[pl.ds(si, 1), :], dst.at[pl.ds(di, 1), :], sem)


def _moe_up_kernel(idx_ref, h_hbm, wg_ref, wu_ref, o_ref, xe, stage, sems):
    rows = xe.shape[0]
    sub = stage.shape[1]
    nbatch = rows // sub

    @pl.when(pl.program_id(1) == 0)
    def _():
        def start(bi):
            slot = bi % 2

            def issue(j, carry):
                _row_copy(h_hbm, idx_ref[0, 0, bi * sub + j], stage.at[slot], j, sems.at[slot]).start()
                return carry
            lax.fori_loop(0, sub, issue, 0)

        start(0)
        for bi in range(nbatch):
            slot = bi % 2
            if bi + 1 < nbatch:
                start(bi + 1)
            pltpu.make_async_copy(h_hbm.at[pl.ds(0, sub), :], stage.at[slot], sems.at[slot]).wait()
            xe[pl.ds(bi * sub, sub), :] = stage[slot].astype(BF16)

    x = xe[...]
    hg = jnp.dot(x, wg_ref[0].astype(BF16), preferred_element_type=F32)
    hu = jnp.dot(x, wu_ref[0].astype(BF16), preferred_element_type=F32)
    o_ref[...] = (hg * _sigmoid(hg) * hu).astype(BF16)


def _moe_up_call(idx3, h_all, w_gate, w_up):
    rows = idx3.shape[2]
    tf = EXPERT_TF
    assert rows % GATHER_ROWS == 0
    return pl.pallas_call(
        _moe_up_kernel,
        out_shape=jax.ShapeDtypeStruct((N_EXPERTS * rows, D_FF), BF16),
        grid=(N_EXPERTS, D_FF // tf),
        in_specs=[pl.BlockSpec((1, 1, rows), lambda e, f: (e, 0, 0), memory_space=pltpu.SMEM),
                  pl.BlockSpec(memory_space=pl.ANY),
                  pl.BlockSpec((1, D_MODEL, tf), lambda e, f: (e, 0, f)),
                  pl.BlockSpec((1, D_MODEL, tf), lambda e, f: (e, 0, f))],
        out_specs=pl.BlockSpec((rows, tf), lambda e, f: (e, f)),
        scratch_shapes=[pltpu.VMEM((rows, D_MODEL), BF16),
                        pltpu.VMEM((2, GATHER_ROWS, D_MODEL), F32),
                        pltpu.SemaphoreType.DMA((2,))],
        compiler_params=_cparams(("arbitrary", "arbitrary"), VMEM_LIMIT_BIG),
        name="moe_up",
    )(idx3, h_all, w_gate, w_up)


def _moe_down_kernel(gate_ref, hh_hbm, wd_ref, o_ref, hbuf, sem):
    rows = hbuf.shape[0]

    @pl.when(pl.program_id(1) == 0)
    def _():
        src = hh_hbm.at[pl.ds(pl.program_id(0) * rows, rows), :]
        cp = pltpu.make_async_copy(src, hbuf, sem)
        cp.start()
        cp.wait()

    o_ref[...] = jnp.dot(hbuf[...], wd_ref[0].astype(BF16), preferred_element_type=F32) * gate_ref[0]


def _moe_down_call(gate3, hh, w_down):
    rows = gate3.shape[1]
    tn = EXPERT_TN
    return pl.pallas_call(
        _moe_down_kernel,
        out_shape=jax.ShapeDtypeStruct((N_EXPERTS * rows, D_MODEL), F32),
        grid=(N_EXPERTS, D_MODEL // tn),
        in_specs=[pl.BlockSpec((1, rows, 1), lambda e, n: (e, 0, 0)),
                  pl.BlockSpec(memory_space=pl.ANY),
                  pl.BlockSpec((1, D_FF, tn), lambda e, n: (e, 0, n))],
        out_specs=pl.BlockSpec((rows, tn), lambda e, n: (e, n)),
        scratch_shapes=[pltpu.VMEM((rows, D_FF), BF16), pltpu.SemaphoreType.DMA],
        compiler_params=_cparams(("arbitrary", "arbitrary"), VMEM_LIMIT_BIG),
        name="moe_down",
    )(gate3, hh, w_down)


def _moe_scatter_kernel(idx_ref, ye_ref, acc_in, acc_hbm, rbuf, sem_r, sem_w):
    del acc_in
    rows = rbuf.shape[0]

    def issue_r(j, carry):
        _row_copy(acc_hbm, idx_ref[0, 0, j], rbuf, j, sem_r).start()
        return carry
    lax.fori_loop(0, rows, issue_r, 0)
    pltpu.make_async_copy(acc_hbm.at[pl.ds(0, rows), :], rbuf, sem_r).wait()
    rbuf[...] = rbuf[...] + ye_ref[...]

    def issue_w(j, carry):
        _row_copy(rbuf, j, acc_hbm, idx_ref[0, 0, j], sem_w).start()
        return carry
    lax.fori_loop(0, rows, issue_w, 0)
    pltpu.make_async_copy(rbuf, acc_hbm.at[pl.ds(0, rows), :], sem_w).wait()


def _moe_scatter_call(idx_blocks, ye, acc):
    nblk, _, rows = idx_blocks.shape
    return pl.pallas_call(
        _moe_scatter_kernel,
        out_shape=jax.ShapeDtypeStruct(acc.shape, F32),
        grid=(nblk,),
        in_specs=[pl.BlockSpec((1, 1, rows), lambda i: (i, 0, 0), memory_space=pltpu.SMEM),
                  pl.BlockSpec((rows, D_MODEL), lambda i: (i, 0)),
                  pl.BlockSpec(memory_space=pl.ANY)],
        out_specs=pl.BlockSpec(memory_space=pl.ANY),
        scratch_shapes=[pltpu.VMEM((rows, D_MODEL), F32),
                        pltpu.SemaphoreType.DMA,
                        pltpu.SemaphoreType.DMA],
        input_output_aliases={2: 0},
        compiler_params=_cparams(("arbitrary",), VMEM_LIMIT_MID),
        name="moe_scatter",
    )(idx_blocks, ye, acc)


def _moe_call(idx, gate, h_all, w_gate, w_up, w_down):
    ne, rows = idx.shape
    assert rows % SCATTER_ROWS == 0
    hh = _moe_up_call(idx.reshape(ne, 1, rows), h_all, w_gate, w_up)
    ye = _moe_down_call(gate.reshape(ne, rows, 1), hh, w_down)
    acc0 = jnp.zeros(h_all.shape, F32)
    return _moe_scatter_call(idx.reshape(ne * rows // SCATTER_ROWS, 1, SCATTER_ROWS), ye, acc0)


def _final_kernel(x_ref, acc_ref, gate_ref, g_ref, o_ref):
    x = x_ref[...] + gate_ref[0] * acc_ref[...]
    ms = jnp.mean(x * x, axis=-1, keepdims=True)
    o_ref[...] = x * lax.rsqrt(ms + RMS_EPS) * g_ref[...]


def _final_call(x1, acc, modr, final_g, rows_per_mod, mod_base, row_off):
    tk = x1.shape[0]
    tm = 256
    blk_off = row_off // tm
    return pl.pallas_call(
        _final_kernel,
        out_shape=jax.ShapeDtypeStruct((tk, D_MODEL), F32),
        grid=(tk // tm,),
        in_specs=[pl.BlockSpec((tm, D_MODEL), lambda i: (i, 0)),
                  pl.BlockSpec((tm, D_MODEL), lambda i: (i + blk_off, 0)),
                  pl.BlockSpec((1, 1, D_MODEL), _mod_row_map(tm, rows_per_mod, mod_base, 5)),
                  pl.BlockSpec((1, D_MODEL), lambda i: (0, 0))],
        out_specs=pl.BlockSpec((tm, D_MODEL), lambda i: (i, 0)),
        compiler_params=_cparams(("parallel",), VMEM_LIMIT_MID),
        name="final_norm",
    )(x1, acc, modr, final_g)


def _layer_to_routing(x3, modr, mod_base, rows_per_mod, s0, p, latent, h_all, row_off, total_rows):
    nb, seq, _ = x3.shape
    tk = nb * seq
    x = x3.reshape(tk, D_MODEL)

    z = _inproj_call(x, p['norm1_g'], modr, p['w_in_bf'], rows_per_mod, mod_base)
    zs = _conv_call(z.reshape(nb, seq, D_IN), p['conv_w'], latent).reshape(tk, C_SHIFT)
    lw, a, g = _prep_call(zs, p['w0'], p['w_lora_up'], p['a0'], p['a_up'], p['g_up'])
    y2, s_fin = _wkv_call(zs, lw, a, p['k_k'], p['k_a'], s0, nb, seq)
    ya = _rwkv_out_call(y2, zs, a, g, p['k_a'], p['r_k'], p['ln_x_g'], p['ln_x_b'])
    yb = _sgu_call(z, p['sgu_ln_g'], p['sgu_ln_b'], p['sgu_w'], p['sgu_b3'])
    x1 = _outproj_call(ya, yb, p['w_out_bf'], x, modr, rows_per_mod, mod_base)

    h_all, probs_t = _norm2_call(x1, p['norm2_g'], modr, p['w_router_t'], rows_per_mod, mod_base,
                                 h_all, row_off, total_rows)
    cap = CAPACITY_FACTOR * tk // N_EXPERTS
    idx, gate = _route_call(probs_t, cap)
    return x1, s_fin, h_all, idx.reshape(N_EXPERTS, cap) + row_off, gate.reshape(N_EXPERTS, cap)


def _block_diag_state(s):
    nb = s.shape[0]
    s6 = s.reshape(nb, 2, N_PAIRS_TOTAL, 2, HEAD_A, HEAD_A)
    eye = jnp.eye(2, dtype=s.dtype)
    bd = jnp.einsum('bdpqvk,qr->bdpqvrk', s6, eye)
    return bd.reshape(nb, 2, N_PAIRS_TOTAL, PAIR, PAIR)


def kernel(x_prompt, x_sample, state_wkv, c, c_ctx, norm1_g, norm2_g, w_mod, b_mod, w_in, conv_w, w0, w_lora_up, a0, a_up, g_up, k_k, k_a, r_k, ln_x_g, ln_x_b, sgu_ln_g, sgu_ln_b, sgu_w, sgu_b, w_out, w_router, w_exp_gate, w_exp_up, w_exp_down, final_g):
    depth = w_in.shape[0]
    nb_c = x_prompt.shape[0]
    nb_l = x_sample.shape[0]
    mod_rows = 8
    cvec = jnp.concatenate([c_ctx[None, :], c, jnp.zeros((mod_rows - 1 - nb_l, D_MODEL), F32)], axis=0)

    xc, xl = x_prompt, x_sample
    new_states = []
    for l in range(depth):
        p = {
            'norm1_g': norm1_g[l][None, :], 'norm2_g': norm2_g[l][None, :],
            'w_in_bf': w_in[l].astype(BF16), 'conv_w': conv_w[l],
            'w0': w0[l], 'w_lora_up': w_lora_up[l].reshape(2 * LORA_W, D_A),
            'a0': a0[l], 'a_up': a_up[l].reshape(2 * LORA_A, D_A), 'g_up': g_up[l],
            'k_k': k_k[l][None, :], 'k_a': k_a[l][None, :], 'r_k': r_k[l][None, :],
            'ln_x_g': ln_x_g[l][None, :], 'ln_x_b': ln_x_b[l][None, :],
            'sgu_ln_g': sgu_ln_g[l][None, :], 'sgu_ln_b': sgu_ln_b[l][None, :],
            'sgu_w': sgu_w[l], 'sgu_b3': sgu_b[l][:, :, None],
            'w_out_bf': w_out[l].astype(BF16), 'w_router_t': w_router[l].T,
            'w_exp_gate': w_exp_gate[l], 'w_exp_up': w_exp_up[l], 'w_exp_down': w_exp_down[l],
            'final_g': final_g[None, :],
        }
        mod = _mod_call(cvec, w_mod[l], b_mod[l][None, :])
        modr = mod.reshape(mod_rows * N_MOD, 1, D_MODEL)

        tk_c = nb_c * xc.shape[1]
        tk_l = nb_l * xl.shape[1]
        total = tk_c + tk_l
        s0_c = jnp.zeros((nb_c, 2, N_PAIRS_TOTAL, PAIR, PAIR), F32)
        x1c, s_fin, h_all, idx_c, gate_c = _layer_to_routing(
            xc, modr, 0, tk_c, s0_c, p, False, None, 0, total)
        new_states.append(s_fin.astype(x_prompt.dtype))
        s0_l = _block_diag_state(state_wkv[:, l].astype(F32))
        x1l, _, h_all, idx_l, gate_l = _layer_to_routing(
            xl, modr, 1, xl.shape[1], s0_l, p, True, h_all, tk_c, total)

        acc = _moe_call(jnp.concatenate([idx_c, idx_l], axis=1), jnp.concatenate([gate_c, gate_l], axis=1),
                        h_all, p['w_exp_gate'], p['w_exp_up'], p['w_exp_down'])
        xc = _final_call(x1c, acc, modr, p['final_g'], tk_c, 0, 0).reshape(xc.shape)
        xl = _final_call(x1l, acc, modr, p['final_g'], xl.shape[1], 1, tk_c).reshape(xl.shape)

    assert depth == 1
    state_new = jnp.stack(new_states, axis=1)
    return (xc, xl, state_new)
```

```python
import functools

import jax
import jax.numpy as jnp
from jax import lax
from jax.experimental import pallas as pl
from jax.experimental.pallas import tpu as pltpu

F32 = jnp.float32
BF16 = jnp.bfloat16

D_MODEL = 4096
D_A = 2048
HEAD_A = 64
H_A = 32
D_B = 2048
G_B = 8
CH_B = 256
SGU_CHUNK = 128
LORA_W = 64
LORA_A = 64
LORA_G = 256
C_SHIFT = 3 * D_A + 2 * LORA_W + 2 * LORA_A + LORA_G
D_IN = C_SHIFT + 2 * D_B
N_EXPERTS = 16
CAPACITY_FACTOR = 2
D_FF = 5632
N_MOD = 6
GRID_W = 64
RMS_EPS = 1e-6
GN_EPS = 64e-5
LN_EPS = 1e-5

V7X_LANES = 128
V7X_SUBLANES = 8
V7X_VMEM_BYTES = 64 * 1024 * 1024
VMEM_LIMIT_BIG = 52 * 1024 * 1024
VMEM_LIMIT_MID = 40 * 1024 * 1024

WKV_CHUNK = 64
PAIR = 2 * HEAD_A
WKV_GW = 2048
N_PAIRS_TOTAL = D_A // PAIR


def _cparams(sem, vmem=None):
    return pltpu.CompilerParams(dimension_semantics=sem, vmem_limit_bytes=vmem)


def _bdot(a, b):
    return jnp.dot(a.astype(BF16), b.astype(BF16), preferred_element_type=F32)


def _nt(a, b):
    return lax.dot_general(a.astype(BF16), b.astype(BF16), (((1,), (1,)), ((), ())),
                           preferred_element_type=F32)


def _tn(a, b):
    return lax.dot_general(a.astype(BF16), b.astype(BF16), (((0,), (0,)), ((), ())),
                           preferred_element_type=F32)


def _split2(x):
    hi = x.astype(BF16)
    lo = (x - hi.astype(F32)).astype(BF16)
    return hi, lo


def _split3(x):
    hi = x.astype(BF16)
    r1 = x - hi.astype(F32)
    mid = r1.astype(BF16)
    lo = (r1 - mid.astype(F32)).astype(BF16)
    return hi, mid, lo


def _dot3(a, b):
    ah, al = _split2(a)
    bh, bl = _split2(b)
    d = functools.partial(jnp.dot, preferred_element_type=F32)
    return d(ah, bh) + (d(al, bh) + d(ah, bl))


def _dot_exact_lhs(a_bf, b):
    bh, bm, bl = _split3(b)
    d = functools.partial(jnp.dot, preferred_element_type=F32)
    return d(a_bf, bh) + (d(a_bf, bm) + d(a_bf, bl))


def _dot_exact_rhs(a, b_bf):
    ah, al = _split2(a)
    d = functools.partial(jnp.dot, preferred_element_type=F32)
    return d(ah, b_bf) + d(al, b_bf)


def _sigmoid(x):
    return jax.nn.sigmoid(x)


def _mod_kernel(c_ref, w_ref, b_ref, o_ref):
    cv = c_ref[...]
    s = cv * _sigmoid(cv)
    o_ref[...] = _dot3(s, w_ref[...]) + b_ref[...]


def _mod_call(cvec, w_mod, b_mod):
    rows, d = cvec.shape
    n = w_mod.shape[1]
    tn = 512
    return pl.pallas_call(
        _mod_kernel,
        out_shape=jax.ShapeDtypeStruct((rows, n), F32),
        grid=(n // tn,),
        in_specs=[pl.BlockSpec((rows, d), lambda j: (0, 0)),
                  pl.BlockSpec((d, tn), lambda j: (0, j)),
                  pl.BlockSpec((1, tn), lambda j: (0, j))],
        out_specs=pl.BlockSpec((rows, tn), lambda j: (0, j)),
        compiler_params=_cparams(("parallel",), VMEM_LIMIT_MID),
        name="mod_proj",
    )(cvec, w_mod, b_mod)


def _inproj_kernel(x_ref, g_ref, sc_ref, sh_ref, w_ref, o_ref, h_scr):
    @pl.when(pl.program_id(1) == 0)
    def _():
        x = x_ref[...]
        ms = jnp.mean(x * x, axis=-1, keepdims=True)
        y = x * lax.rsqrt(ms + RMS_EPS) * g_ref[...]
        h_scr[...] = (y * (1.0 + sc_ref[0]) + sh_ref[0]).astype(BF16)

    o_ref[...] = jnp.dot(h_scr[...], w_ref[...], preferred_element_type=F32)


def _mod_row_map(tm, rows_per_mod, mod_base, comp):
    def index_map(i, *_):
        return ((mod_base + (i * tm) // rows_per_mod) * N_MOD + comp, 0, 0)
    return index_map


def _inproj_call(x, norm_g, modr, w_bf, rows_per_mod, mod_base):
    tk, d = x.shape
    n = w_bf.shape[1]
    tm, tn = 512, 768
    return pl.pallas_call(
        _inproj_kernel,
        out_shape=jax.ShapeDtypeStruct((tk, n), F32),
        grid=(tk // tm, n // tn),
        in_specs=[pl.BlockSpec((tm, d), lambda i, j: (i, 0)),
                  pl.BlockSpec((1, d), lambda i, j: (0, 0)),
                  pl.BlockSpec((1, 1, d), _mod_row_map(tm, rows_per_mod, mod_base, 1)),
                  pl.BlockSpec((1, 1, d), _mod_row_map(tm, rows_per_mod, mod_base, 0)),
                  pl.BlockSpec((d, tn), lambda i, j: (0, j))],
        out_specs=pl.BlockSpec((tm, tn), lambda i, j: (i, j)),
        scratch_shapes=[pltpu.VMEM((tm, d), BF16)],
        compiler_params=_cparams(("parallel", "arbitrary"), VMEM_LIMIT_BIG),
        name="in_proj",
    )(x, norm_g, modr, modr, w_bf)


CONV_PAD = 72
CONV_ROWS = 256


def _conv_kernel(z_ref, w_ref, o_ref, ext, *, seq, width, vertical):
    bb = z_ref.shape[0]
    tc = z_ref.shape[2]
    zeros = jnp.zeros((CONV_PAD, tc), F32)
    ext[pl.ds(0, CONV_PAD), :] = zeros
    ext[pl.ds(CONV_PAD + seq, CONV_PAD), :] = zeros
    rows = min(CONV_ROWS, seq)
    col = lax.broadcasted_iota(jnp.int32, (rows, tc), 0)
    for b in range(bb):
        ext[pl.ds(CONV_PAD, seq), :] = z_ref[b]
        for r0 in range(0, seq, rows):
            cpos = (col + r0) % width
            acc = None
            for dj in (-1, 0, 1):
                part = None
                for di in ((-1, 0, 1) if vertical else (0,)):
                    tap = ext[pl.ds(CONV_PAD + r0 + di * width + dj, rows), :]
                    term = tap * w_ref[di + 1, pl.ds(dj + 1, 1), :]
                    part = term if part is None else part + term
                if vertical and dj == -1:
                    part = jnp.where(cpos == 0, 0.0, part)
                elif vertical and dj == 1:
                    part = jnp.where(cpos == width - 1, 0.0, part)
                acc = part if acc is None else acc + part
            o_ref[b, pl.ds(r0, rows), :] = acc


def _conv_call(z3, conv_w, latent):
    nb, seq, _ = z3.shape
    if latent:
        bb, tc, width = 1, 256, GRID_W
    else:
        bb, tc, width = 4, 512, seq
    kern = functools.partial(_conv_kernel, seq=seq, width=width, vertical=latent)
    return pl.pallas_call(
        kern,
        out_shape=jax.ShapeDtypeStruct((nb, seq, C_SHIFT), F32),
        grid=(nb // bb, C_SHIFT // tc),
        in_specs=[pl.BlockSpec((bb, seq, tc), lambda b, j: (b, 0, j)),
                  pl.BlockSpec((3, 3, tc), lambda b, j: (0, 0, j))],
        out_specs=pl.BlockSpec((bb, seq, tc), lambda b, j: (b, 0, j)),
        scratch_shapes=[pltpu.VMEM((seq + 2 * CONV_PAD, tc), F32)],
        compiler_params=_cparams(("parallel", "parallel"), VMEM_LIMIT_MID),
        name="short_conv",
    )(z3, conv_w)


def _prep_kernel(zl_ref, w0_ref, wup_ref, a0_ref, aup_ref, gup_ref, lw_ref, a_ref, g_ref):
    zl = zl_ref[...]
    tm = zl.shape[0]
    wd = jnp.tanh(zl[:, 0:2 * LORA_W])
    ad = zl[:, 2 * LORA_W:2 * LORA_W + 2 * LORA_A]
    gd = _sigmoid(zl[:, 2 * LORA_W + 2 * LORA_A:])
    lane = lax.broadcasted_iota(jnp.int32, (tm, 2 * LORA_W), 1)
    for d in range(2):
        sel = (lane < LORA_W) if d == 0 else (lane >= LORA_W)
        wlin = w0_ref[pl.ds(d, 1), :] + _dot3(jnp.where(sel, wd, 0.0), wup_ref[...])
        w_log = -jax.nn.softplus(-wlin) - 0.5
        lw_ref[d] = -jnp.exp(w_log)
        alin = a0_ref[pl.ds(d, 1), :] + _dot3(jnp.where(sel, ad, 0.0), aup_ref[...])
        a_ref[d] = _sigmoid(alin)
    g_ref[...] = _dot3(gd, gup_ref[...])


def _prep_call(zs, w0, wup, a0, aup, gup):
    tk = zs.shape[0]
    tm = 256
    lblk = (3 * D_A) // 512
    return pl.pallas_call(
        _prep_kernel,
        out_shape=[jax.ShapeDtypeStruct((2, tk, D_A), F32),
                   jax.ShapeDtypeStruct((2, tk, D_A), F32),
                   jax.ShapeDtypeStruct((tk, D_A), F32)],
        grid=(tk // tm,),
        in_specs=[pl.BlockSpec((tm, 512), lambda i: (i, lblk)),
                  pl.BlockSpec((2, D_A), lambda i: (0, 0)),
                  pl.BlockSpec((2 * LORA_W, D_A), lambda i: (0, 0)),
                  pl.BlockSpec((2, D_A), lambda i: (0, 0)),
                  pl.BlockSpec((2 * LORA_A, D_A), lambda i: (0, 0)),
                  pl.BlockSpec((LORA_G, D_A), lambda i: (0, 0))],
        out_specs=[pl.BlockSpec((2, tm, D_A), lambda i: (0, i, 0)),
                   pl.BlockSpec((2, tm, D_A), lambda i: (0, i, 0)),
                   pl.BlockSpec((tm, D_A), lambda i: (i, 0))],
        compiler_params=_cparams(("parallel",), VMEM_LIMIT_MID),
        name="rwkv_prep",
    )(zs, w0, wup, a0, aup, gup)


def _wkv_kernel(r_ref, k_ref, v_ref, lw_ref, a_ref, kk_ref, ka_ref, s0_ref, y_ref, sout_ref, s_scr):
    d = pl.program_id(1)
    ci = pl.program_id(3)
    nci = pl.num_programs(3)
    C = r_ref.shape[0]
    gw = r_ref.shape[1]
    n_pairs = gw // PAIR

    @pl.when(ci == 0)
    def _():
        s_scr[...] = s0_ref[0, 0]

    rev = d == 1
    ri = lax.broadcasted_iota(jnp.int32, (C, C), 0)
    cj = lax.broadcasted_iota(jnp.int32, (C, C), 1)
    before = jnp.where(rev, cj - ri, ri - cj) >= 0
    tri = jnp.where(before, 1.0, 0.0).astype(BF16)

    lw = lw_ref[0]
    cum = _dot_exact_lhs(tri, lw)
    total = jnp.where(rev, cum[0:1, :], cum[C - 1:C, :])

    r2 = lax.broadcasted_iota(jnp.int32, (2 * C, 2 * C), 0)
    c2 = lax.broadcasted_iota(jnp.int32, (2 * C, 2 * C), 1)
    same = (r2 // C) == (c2 // C)
    dd = jnp.where(rev, (c2 % C) - (r2 % C), (r2 % C) - (c2 % C))
    strict = same & (dd > 0)
    incl = same & (dd >= 0)
    eye = jnp.where(r2 == c2, 1.0, 0.0)
    merge_masks = []
    s = 1
    while s < C:
        merge_masks.append(((r2 // (2 * s)) == (c2 // (2 * s))) & ((r2 // s) != (c2 // s)))
        s *= 2
    lane = lax.broadcasted_iota(jnp.int32, (C, PAIR), 1)
    first = lane < HEAD_A
    hr = lax.broadcasted_iota(jnp.int32, (PAIR, PAIR), 0) // HEAD_A
    hc = lax.broadcasted_iota(jnp.int32, (PAIR, PAIR), 1) // HEAD_A
    head_ones = jnp.where(hr == hc, 1.0, 0.0).astype(BF16)

    def pairs(x):
        return jnp.stack([x[:, p * PAIR:(p + 1) * PAIR] for p in range(n_pairs)], axis=0)

    def stack(x):
        return jnp.concatenate([jnp.where(first, x, 0.0), jnp.where(first, 0.0, x)], axis=1)

    def bnn(x, y):
        return lax.dot_general(x.astype(BF16), y.astype(BF16), (((2,), (1,)), ((0,), (0,))),
                               preferred_element_type=F32)

    def bnt(x, y):
        return lax.dot_general(x.astype(BF16), y.astype(BF16), (((2,), (2,)), ((0,), (0,))),
                               preferred_element_type=F32)

    r = pairs(r_ref[...])
    k = pairs(k_ref[...])
    v = pairs(v_ref[...])
    a = pairs(a_ref[0])
    lwp = pairs(lw)
    cp = pairs(cum)
    tot = pairs(total)
    kkr = k * pairs(kk_ref[...])
    n2 = _dot_exact_rhs((kkr * kkr).reshape(n_pairs * C, PAIR), head_ones).reshape(n_pairs, C, PAIR)
    kk = kkr / jnp.maximum(jnp.sqrt(n2), 1e-12)
    kd = k * (1.0 + (a - 1.0) * pairs(ka_ref[...]))
    b = kk * a
    e_in = jnp.exp(cp)
    e_ex = jnp.exp(cp - lwp)
    e_ng = jnp.exp(-cp)
    e_tot = jnp.exp(tot - cp)
    s_a = stack(kk * e_ex)
    s_r = stack(r * e_in)
    s_b = stack(b * e_ng)
    s_k = stack(kd * e_ng)
    s_kp = stack(kd * e_tot)
    s_bp = stack(b * e_tot)
    s_v = stack(v)

    g = bnt(jnp.concatenate([s_a, s_r], axis=1), jnp.concatenate([s_b, s_k], axis=1))
    n = jnp.where(strict, -g[:, 0:2 * C, 0:2 * C], 0.0)
    m_ak = jnp.where(strict, g[:, 0:2 * C, 2 * C:], 0.0)
    m_rb = jnp.where(incl, g[:, 2 * C:, 0:2 * C], 0.0)
    m_rk = jnp.where(incl, g[:, 2 * C:, 2 * C:], 0.0)

    t = eye + jnp.where(merge_masks[0], n, 0.0)
    for mask in merge_masks[1:]:
        t = t + bnn(t, bnn(jnp.where(mask, n, 0.0), t))

    w = bnn(jnp.concatenate([m_ak, m_rk], axis=1), s_v)
    ta = bnn(t, jnp.concatenate([s_a, w[:, 0:2 * C]], axis=2))
    s_at = ta[:, :, 0:PAIR]
    u_hat = ta[:, :, PAIR:]

    s_old = s_scr[...]
    z = bnt(jnp.concatenate([s_at, s_r], axis=1), s_old)
    u = z[:, 0:2 * C] + u_hat
    y_st = z[:, 2 * C:] + w[:, 2 * C:] - bnn(m_rb, u)
    y = y_st[:, 0:C] + y_st[:, C:]
    decay = jnp.exp(tot)
    for p in range(n_pairs):
        y_ref[0, :, p * PAIR:(p + 1) * PAIR] = y[p]
        s_scr[p] = s_old[p] * decay[p] + _tn(jnp.concatenate([s_v[p], u[p]], axis=0),
                                             jnp.concatenate([s_kp[p], -s_bp[p]], axis=0))

    @pl.when(ci == nci - 1)
    def _():
        for p in range(n_pairs):
            s_pair = s_scr[p]
            sout_ref[0, 0, 2 * p] = s_pair[0:HEAD_A, 0:HEAD_A]
            sout_ref[0, 0, 2 * p + 1] = s_pair[HEAD_A:, HEAD_A:]


def _wkv_call(zs, lw, a, k_k, k_a, s0_bd, nb, seq):
    tk = zs.shape[0]
    C = WKV_CHUNK
    nch = seq // C
    gw = WKV_GW
    ng = D_A // gw
    npair = gw // PAIR
    koff = D_A // gw

    def tokmap(b, d, g, c):
        return b * nch + c + d * (nch - 1 - 2 * c)

    return pl.pallas_call(
        _wkv_kernel,
        out_shape=[jax.ShapeDtypeStruct((2, tk, D_A), F32),
                   jax.ShapeDtypeStruct((nb, 2, H_A, HEAD_A, HEAD_A), F32)],
        grid=(nb, 2, ng, nch),
        in_specs=[pl.BlockSpec((C, gw), lambda b, d, g, c: (tokmap(b, d, g, c), g)),
                  pl.BlockSpec((C, gw), lambda b, d, g, c: (tokmap(b, d, g, c), koff + g)),
                  pl.BlockSpec((C, gw), lambda b, d, g, c: (tokmap(b, d, g, c), 2 * koff + g)),
                  pl.BlockSpec((1, C, gw), lambda b, d, g, c: (d, tokmap(b, d, g, c), g)),
                  pl.BlockSpec((1, C, gw), lambda b, d, g, c: (d, tokmap(b, d, g, c), g)),
                  pl.BlockSpec((1, gw), lambda b, d, g, c: (0, g)),
                  pl.BlockSpec((1, gw), lambda b, d, g, c: (0, g)),
                  pl.BlockSpec((1, 1, npair, PAIR, PAIR), lambda b, d, g, c: (b, d, g, 0, 0))],
        out_specs=[pl.BlockSpec((1, C, gw), lambda b, d, g, c: (d, tokmap(b, d, g, c), g)),
                   pl.BlockSpec((1, 1, 2 * npair, HEAD_A, HEAD_A), lambda b, d, g, c: (b, d, g, 0, 0))],
        scratch_shapes=[pltpu.VMEM((npair, PAIR, PAIR), F32)],
        compiler_params=_cparams(("parallel", "parallel", "parallel", "arbitrary"), VMEM_LIMIT_MID),
        name="wkv_scan",
    )(zs, zs, zs, lw, a, k_k, k_a, s0_bd)


def _rwkv_out_kernel(y_ref, r_ref, k_ref, v_ref, a_ref, g_ref, ka_ref, rk_ref, lg_ref, lb_ref, o_ref):
    gw = r_ref.shape[1]
    hr = lax.broadcasted_iota(jnp.int32, (PAIR, PAIR), 0) // HEAD_A
    hc = lax.broadcasted_iota(jnp.int32, (PAIR, PAIR), 1) // HEAD_A
    head_ones = jnp.where(hr == hc, 1.0, 0.0).astype(BF16)
    for p in range(gw // PAIR):
        sl = slice(p * PAIR, (p + 1) * PAIR)
        y = y_ref[0, :, sl] + y_ref[1, :, sl]
        mean = _dot_exact_rhs(y, head_ones) * (1.0 / HEAD_A)
        yc = y - mean
        var = _dot_exact_rhs(yc * yc, head_ones) * (1.0 / HEAD_A)
        yn = yc * lax.rsqrt(var + GN_EPS) * lg_ref[:, sl] + lb_ref[:, sl]
        r = r_ref[:, sl]
        k = k_ref[:, sl]
        ka = ka_ref[:, sl]
        kd0 = k * (1.0 + (a_ref[0, :, sl] - 1.0) * ka)
        kd1 = k * (1.0 + (a_ref[1, :, sl] - 1.0) * ka)
        dotp = _dot_exact_rhs(r * ((kd0 + kd1) * rk_ref[:, sl]), head_ones)
        o_ref[:, sl] = ((yn + dotp * v_ref[:, sl]) * g_ref[:, sl]).astype(BF16)


def _rwkv_out_call(y2, zs, a, g, k_a, r_k, ln_g, ln_b):
    tk = zs.shape[0]
    tm, gw = 256, 512
    koff = D_A // gw
    vec = pl.BlockSpec((1, gw), lambda i, j: (0, j))
    return pl.pallas_call(
        _rwkv_out_kernel,
        out_shape=jax.ShapeDtypeStruct((tk, D_A), BF16),
        grid=(tk // tm, D_A // gw),
        in_specs=[pl.BlockSpec((2, tm, gw), lambda i, j: (0, i, j)),
                  pl.BlockSpec((tm, gw), lambda i, j: (i, j)),
                  pl.BlockSpec((tm, gw), lambda i, j: (i, koff + j)),
                  pl.BlockSpec((tm, gw), lambda i, j: (i, 2 * koff + j)),
                  pl.BlockSpec((2, tm, gw), lambda i, j: (0, i, j)),
                  pl.BlockSpec((tm, gw), lambda i, j: (i, j)),
                  vec, vec, vec, vec],
        out_specs=pl.BlockSpec((tm, gw), lambda i, j: (i, j)),
        compiler_params=_cparams(("parallel", "parallel"), VMEM_LIMIT_MID),
        name="rwkv_out",
    )(y2, zs, zs, zs, a, g, k_a, r_k, ln_g, ln_b)


def _gelu(x):
    return 0.5 * x * (1.0 + lax.erf(x * 0.7071067811865476))


def _sgu_kernel(*refs):
    nblk = (len(refs) - 5) // 2
    zu_refs, zv_refs = refs[:nblk], refs[nblk:2 * nblk]
    lg_ref, lb_ref, ws_ref, bs_ref, o_ref = refs[2 * nblk:]
    tm = o_ref.shape[0]
    per_blk = zu_refs[0].shape[1] // CH_B
    for grp in range(G_B):
        zu_ref, zv_ref = zu_refs[grp // per_blk], zv_refs[grp // per_blk]
        bsl = slice((grp % per_blk) * CH_B, (grp % per_blk + 1) * CH_B)
        sl = slice(grp * CH_B, (grp + 1) * CH_B)
        vf = _gelu(zv_ref[:, bsl])
        mean = jnp.mean(vf, axis=-1, keepdims=True)
        vc = vf - mean
        var = jnp.mean(vc * vc, axis=-1, keepdims=True)
        vn = vc * lax.rsqrt(var + LN_EPS) * lg_ref[:, sl] + lb_ref[:, sl]
        for c0 in range(0, tm, SGU_CHUNK):
            mixed = _bdot(ws_ref[grp], vn[c0:c0 + SGU_CHUNK]) + bs_ref[grp]
            u = _gelu(zu_ref[pl.ds(c0, SGU_CHUNK), bsl])
            o_ref[pl.ds(c0, SGU_CHUNK), sl] = (u * mixed).astype(BF16)


def _sgu_call(z, ln_g, ln_b, w_s, b_s3):
    tk = z.shape[0]
    tm = 256
    bw = 512
    u0 = C_SHIFT // bw
    v0 = (C_SHIFT + D_B) // bw
    nbw = D_B // bw
    in_specs = ([pl.BlockSpec((tm, bw), functools.partial(lambda i, q: (i, q), q=u0 + q)) for q in range(nbw)]
                + [pl.BlockSpec((tm, bw), functools.partial(lambda i, q: (i, q), q=v0 + q)) for q in range(nbw)]
                + [pl.BlockSpec((1, D_B), lambda i: (0, 0)),
                   pl.BlockSpec((1, D_B), lambda i: (0, 0)),
                   pl.BlockSpec((G_B, SGU_CHUNK, SGU_CHUNK), lambda i: (0, 0, 0)),
                   pl.BlockSpec((G_B, SGU_CHUNK, 1), lambda i: (0, 0, 0))])
    return pl.pallas_call(
        _sgu_kernel,
        out_shape=jax.ShapeDtypeStruct((tk, D_B), BF16),
        grid=(tk // tm,),
        in_specs=in_specs,
        out_specs=pl.BlockSpec((tm, D_B), lambda i: (i, 0)),
        compiler_params=_cparams(("parallel",), VMEM_LIMIT_MID),
        name="sgu",
    )(*([z] * (2 * nbw)), ln_g, ln_b, w_s, b_s3)


def _outproj_kernel(ya_ref, yb_ref, wa_ref, wb_ref, x_ref, gate_ref, o_ref):
    acc = jnp.dot(ya_ref[...], wa_ref[...], preferred_element_type=F32)
    acc = acc + jnp.dot(yb_ref[...], wb_ref[...], preferred_element_type=F32)
    o_ref[...] = x_ref[...] + gate_ref[0] * acc


def _outproj_call(ya, yb, w_bf, x, modr, rows_per_mod, mod_base):
    tk = x.shape[0]
    tm, tn = 512, 1024

    def gate_map(i, j):
        return ((mod_base + (i * tm) // rows_per_mod) * N_MOD + 2, 0, j)

    return pl.pallas_call(
        _outproj_kernel,
        out_shape=jax.ShapeDtypeStruct((tk, D_MODEL), F32),
        grid=(tk // tm, D_MODEL // tn),
        in_specs=[pl.BlockSpec((tm, D_A), lambda i, j: (i, 0)),
                  pl.BlockSpec((tm, D_B), lambda i, j: (i, 0)),
                  pl.BlockSpec((D_A, tn), lambda i, j: (0, j)),
                  pl.BlockSpec((D_B, tn), lambda i, j: (1, j)),
                  pl.BlockSpec((tm, tn), lambda i, j: (i, j)),
                  pl.BlockSpec((1, 1, tn), gate_map)],
        out_specs=pl.BlockSpec((tm, tn), lambda i, j: (i, j)),
        compiler_params=_cparams(("parallel", "arbitrary"), VMEM_LIMIT_MID),
        name="out_proj",
    )(ya, yb, w_bf, w_bf, x, modr)


def _norm2_kernel(x_ref, g_ref, sc_ref, sh_ref, wr_ref, *rest):
    h_ref, p_ref = rest[-2:]
    x = x_ref[...]
    ms = jnp.mean(x * x, axis=-1, keepdims=True)
    y = x * lax.rsqrt(ms + RMS_EPS) * g_ref[...]
    h = y * (1.0 + sc_ref[0]) + sh_ref[0]
    h_ref[...] = h
    hh, hl = _split2(h)
    wh, wl = _split2(wr_ref[...])
    nt = functools.partial(lax.dot_general, dimension_numbers=(((1,), (1,)), ((), ())),
                           preferred_element_type=F32)
    logits = nt(wh, hh) + (nt(wh, hl) + nt(wl, hh))
    m = jnp.max(logits, axis=0, keepdims=True)
    e = jnp.exp(logits - m)
    p_ref[...] = e / jnp.sum(e, axis=0, keepdims=True)


def _norm2_call(x1, norm_g, modr, w_router_t, rows_per_mod, mod_base, h_all, row_off, total_rows):
    tk = x1.shape[0]
    tm = 256
    blk_off = row_off // tm
    in_specs = [pl.BlockSpec((tm, D_MODEL), lambda i: (i, 0)),
                pl.BlockSpec((1, D_MODEL), lambda i: (0, 0)),
                pl.BlockSpec((1, 1, D_MODEL), _mod_row_map(tm, rows_per_mod, mod_base, 4)),
                pl.BlockSpec((1, 1, D_MODEL), _mod_row_map(tm, rows_per_mod, mod_base, 3)),
                pl.BlockSpec((N_EXPERTS, D_MODEL), lambda i: (0, 0))]
    args = [x1, norm_g, modr, modr, w_router_t]
    aliases = {}
    if h_all is not None:
        in_specs.append(pl.BlockSpec(memory_space=pl.ANY))
        args.append(h_all)
        aliases = {5: 0}
    return pl.pallas_call(
        _norm2_kernel,
        out_shape=[jax.ShapeDtypeStruct((total_rows, D_MODEL), F32),
                   jax.ShapeDtypeStruct((N_EXPERTS, tk), F32)],
        grid=(tk // tm,),
        in_specs=in_specs,
        out_specs=[pl.BlockSpec((tm, D_MODEL), lambda i: (i + blk_off, 0)),
                   pl.BlockSpec((N_EXPERTS, tm), lambda i: (0, i))],
        input_output_aliases=aliases,
        compiler_params=_cparams(("parallel",), VMEM_LIMIT_MID),
        name="norm2_router",
    )(*args)


ROUTE_SLOTS = 128


def _route_kernel(p_ref, cnt_ref, slot_ref, *, cap):
    ne, nt = p_ref.shape
    cnt_scr, sel_scr = cnt_ref, slot_ref
    p = p_ref[...]
    bits = pltpu.bitcast(p, jnp.int32)

    def count_ge(v):
        return jnp.sum(jnp.where(bits >= v, 1.0, 0.0), axis=1, keepdims=True)

    def bis(it, cur):
        cand = cur | (jnp.int32(1) << (30 - it))
        return jnp.where(count_ge(cand) >= cap, cand, cur)

    thr = lax.fori_loop(0, 31, bis, jnp.zeros((ne, 1), jnp.int32))
    gt = bits > thr
    eq = bits == thr
    need = cap - jnp.sum(jnp.where(gt, 1.0, 0.0), axis=1, keepdims=True)

    lr = lax.broadcasted_iota(jnp.int32, (V7X_LANES, V7X_LANES), 0)
    lc = lax.broadcasted_iota(jnp.int32, (V7X_LANES, V7X_LANES), 1)
    upper = jnp.where(lr <= lc, 1.0, 0.0).astype(BF16)

    def prefix(mask_f32, dst):
        carry = jnp.zeros((ne, 1), F32)
        for c0 in range(0, nt, V7X_LANES):
            blk = jnp.dot(mask_f32[:, c0:c0 + V7X_LANES].astype(BF16), upper,
                          preferred_element_type=F32) + carry
            dst[:, c0:c0 + V7X_LANES] = blk
            carry = blk[:, V7X_LANES - 1:V7X_LANES]

    prefix(jnp.where(eq, 1.0, 0.0), cnt_scr)
    sel = gt | (eq & (cnt_scr[...] <= need))
    prefix(jnp.where(sel, 1.0, 0.0), cnt_scr)
    sel_scr[...] = jnp.where(sel, cnt_scr[...], -1.0)


def _slots_kernel(ends_ref, cnt_ref, slot_ref, p_ref, idx_ref, gate_ref):
    nt = cnt_ref.shape[2]
    nch = nt // V7X_LANES
    s0 = pl.program_id(1) * ROUTE_SLOTS
    slot = (lax.broadcasted_iota(jnp.int32, (ROUTE_SLOTS, V7X_LANES), 0) + s0).astype(F32)

    def bounds(c, carry):
        lo, hi = carry
        end_c = ends_ref[0, 0, c]
        prev_c = jnp.where(c > 0, ends_ref[0, 0, jnp.maximum(c - 1, 0)], 0)
        lo = lo + jnp.where(end_c <= s0, 1, 0)
        hi = hi + jnp.where(prev_c <= s0 + ROUTE_SLOTS - 1, 1, 0)
        return lo, hi

    lo, hi = lax.fori_loop(0, nch, bounds, (jnp.int32(0), jnp.int32(0)))

    def chunk(c, accs):
        acc_i, acc_g = accs
        off = pl.multiple_of(c * V7X_LANES, V7X_LANES)
        c_row = cnt_ref[0, :, pl.ds(off, V7X_LANES)]
        m_row = slot_ref[0, :, pl.ds(off, V7X_LANES)]
        p_row = p_ref[0, :, pl.ds(off, V7X_LANES)]
        acc_i = acc_i + jnp.where(c_row <= slot, 1.0, 0.0)
        acc_g = acc_g + jnp.where(m_row == slot + 1.0, p_row, 0.0)
        return acc_i, acc_g

    zero = jnp.zeros((ROUTE_SLOTS, V7X_LANES), F32)
    acc_i, acc_g = lax.fori_loop(lo, hi, chunk, (zero, zero))
    before = (lo * V7X_LANES).astype(F32)
    idx_ref[0] = (jnp.sum(acc_i, axis=1, keepdims=True) + before).astype(jnp.int32)
    gate_ref[0] = jnp.sum(acc_g, axis=1, keepdims=True)


def _route_call(probs_t, cap):
    ne, nt = probs_t.shape
    assert cap % ROUTE_SLOTS == 0 and nt % V7X_LANES == 0
    cnt, slot = pl.pallas_call(
        functools.partial(_route_kernel, cap=cap),
        out_shape=[jax.ShapeDtypeStruct((ne, nt), F32), jax.ShapeDtypeStruct((ne, nt), F32)],
        compiler_params=_cparams(None, VMEM_LIMIT_MID),
        name="route_select",
    )(probs_t)
    nch = nt // V7X_LANES
    ends = cnt.reshape(ne, nch, V7X_LANES)[:, :, V7X_LANES - 1].astype(jnp.int32)
    row = pl.BlockSpec((1, 1, nt), lambda e, s: (e, 0, 0))
    return pl.pallas_call(
        _slots_kernel,
        out_shape=[jax.ShapeDtypeStruct((ne, cap, 1), jnp.int32),
                   jax.ShapeDtypeStruct((ne, cap, 1), F32)],
        grid=(ne, cap // ROUTE_SLOTS),
        in_specs=[pl.BlockSpec((1, 1, nch), lambda e, s: (e, 0, 0), memory_space=pltpu.SMEM), row, row, row],
        out_specs=[pl.BlockSpec((1, ROUTE_SLOTS, 1), lambda e, s: (e, s, 0)),
                   pl.BlockSpec((1, ROUTE_SLOTS, 1), lambda e, s: (e, s, 0))],
        compiler_params=_cparams(("parallel", "parallel"), VMEM_LIMIT_MID),
        name="route_slots",
    )(ends.reshape(ne, 1, nch), cnt.reshape(ne, 1, nt), slot.reshape(ne, 1, nt), probs_t.reshape(ne, 1, nt))


EXPERT_TF = 256
EXPERT_TN = 256
GATHER_ROWS = 256
SCATTER_ROWS = 512
DMA_ISSUE_UNROLL = 8


def _row_copy(src, si, dst, di, sem):
    return pltpu.make_async_copy(src.at[pl.ds(si, 1), :], dst.at[pl.ds(di, 1), :], sem)


def _moe_up_kernel(idx_ref, h_hbm, wg_ref, wu_ref, o_ref, xe, stage, sems):
    rows = xe.shape[0]
    sub = stage.shape[1]
    nbatch = rows // sub

    @pl.when(pl.program_id(1) == 0)
    def _():
        def start(bi):
            slot = bi % 2

            def issue(j, carry):
                _row_copy(h_hbm, idx_ref[0, 0, bi * sub + j], stage.at[slot], j, sems.at[slot]).start()
                return carry
            lax.fori_loop(0, sub, issue, 0, unroll=DMA_ISSUE_UNROLL)

        start(0)
        for bi in range(nbatch):
            slot = bi % 2
            if bi + 1 < nbatch:
                start(bi + 1)
            pltpu.make_async_copy(h_hbm.at[pl.ds(0, sub), :], stage.at[slot], sems.at[slot]).wait()
            xe[pl.ds(bi * sub, sub), :] = stage[slot].astype(BF16)

    x = xe[...]
    hg = jnp.dot(x, wg_ref[0].astype(BF16), preferred_element_type=F32)
    hu = jnp.dot(x, wu_ref[0].astype(BF16), preferred_element_type=F32)
    o_ref[...] = (hg * _sigmoid(hg) * hu).astype(BF16)


def _moe_up_call(idx3, h_all, w_gate, w_up):
    rows = idx3.shape[2]
    tf = EXPERT_TF
    assert rows % GATHER_ROWS == 0
    return pl.pallas_call(
        _moe_up_kernel,
        out_shape=jax.ShapeDtypeStruct((N_EXPERTS * rows, D_FF), BF16),
        grid=(N_EXPERTS, D_FF // tf),
        in_specs=[pl.BlockSpec((1, 1, rows), lambda e, f: (e, 0, 0), memory_space=pltpu.SMEM),
                  pl.BlockSpec(memory_space=pl.ANY),
                  pl.BlockSpec((1, D_MODEL, tf), lambda e, f: (e, 0, f)),
                  pl.BlockSpec((1, D_MODEL, tf), lambda e, f: (e, 0, f))],
        out_specs=pl.BlockSpec((rows, tf), lambda e, f: (e, f)),
        scratch_shapes=[pltpu.VMEM((rows, D_MODEL), BF16),
                        pltpu.VMEM((2, GATHER_ROWS, D_MODEL), F32),
                        pltpu.SemaphoreType.DMA((2,))],
        compiler_params=_cparams(("arbitrary", "arbitrary"), VMEM_LIMIT_BIG),
        name="moe_up",
    )(idx3, h_all, w_gate, w_up)


def _moe_down_kernel(gate_ref, hh_hbm, wd_ref, o_ref, hbuf, sem):
    rows = hbuf.shape[0]

    @pl.when(pl.program_id(1) == 0)
    def _():
        src = hh_hbm.at[pl.ds(pl.program_id(0) * rows, rows), :]
        cp = pltpu.make_async_copy(src, hbuf, sem)
        cp.start()
        cp.wait()

    o_ref[...] = jnp.dot(hbuf[...], wd_ref[0].astype(BF16), preferred_element_type=F32) * gate_ref[0]


def _moe_down_call(gate3, hh, w_down):
    rows = gate3.shape[1]
    tn = EXPERT_TN
    return pl.pallas_call(
        _moe_down_kernel,
        out_shape=jax.ShapeDtypeStruct((N_EXPERTS * rows, D_MODEL), F32),
        grid=(N_EXPERTS, D_MODEL // tn),
        in_specs=[pl.BlockSpec((1, rows, 1), lambda e, n: (e, 0, 0)),
                  pl.BlockSpec(memory_space=pl.ANY),
                  pl.BlockSpec((1, D_FF, tn), lambda e, n: (e, 0, n))],
        out_specs=pl.BlockSpec((rows, tn), lambda e, n: (e, n)),
        scratch_shapes=[pltpu.VMEM((rows, D_FF), BF16), pltpu.SemaphoreType.DMA],
        compiler_params=_cparams(("arbitrary", "arbitrary"), VMEM_LIMIT_BIG),
        name="moe_down",
    )(gate3, hh, w_down)


def _moe_scatter_kernel(idx_ref, ye_ref, acc_in, acc_hbm, rbuf, sem_r, sem_w):
    del acc_in
    rows = rbuf.shape[0]

    def issue_r(j, carry):
        _row_copy(acc_hbm, idx_ref[0, 0, j], rbuf, j, sem_r).start()
        return carry
    lax.fori_loop(0, rows, issue_r, 0, unroll=DMA_ISSUE_UNROLL)
    pltpu.make_async_copy(acc_hbm.at[pl.ds(0, rows), :], rbuf, sem_r).wait()
    rbuf[...] = rbuf[...] + ye_ref[...]

    def issue_w(j, carry):
        _row_copy(rbuf, j, acc_hbm, idx_ref[0, 0, j], sem_w).start()
        return carry
    lax.fori_loop(0, rows, issue_w, 0, unroll=DMA_ISSUE_UNROLL)
    pltpu.make_async_copy(rbuf, acc_hbm.at[pl.ds(0, rows), :], sem_w).wait()


def _moe_scatter_call(idx_blocks, ye, acc):
    nblk, _, rows = idx_blocks.shape
    return pl.pallas_call(
        _moe_scatter_kernel,
        out_shape=jax.ShapeDtypeStruct(acc.shape, F32),
        grid=(nblk,),
        in_specs=[pl.BlockSpec((1, 1, rows), lambda i: (i, 0, 0), memory_space=pltpu.SMEM),
                  pl.BlockSpec((rows, D_MODEL), lambda i: (i, 0)),
                  pl.BlockSpec(memory_space=pl.ANY)],
        out_specs=pl.BlockSpec(memory_space=pl.ANY),
        scratch_shapes=[pltpu.VMEM((rows, D_MODEL), F32),
                        pltpu.SemaphoreType.DMA,
                        pltpu.SemaphoreType.DMA],
        input_output_aliases={2: 0},
        compiler_params=_cparams(("arbitrary",), VMEM_LIMIT_MID),
        name="moe_scatter",
    )(idx_blocks, ye, acc)


def _moe_call(idx, gate, h_all, w_gate, w_up, w_down):
    ne, rows = idx.shape
    assert rows % SCATTER_ROWS == 0
    hh = _moe_up_call(idx.reshape(ne, 1, rows), h_all, w_gate, w_up)
    ye = _moe_down_call(gate.reshape(ne, rows, 1), hh, w_down)
    acc0 = jnp.zeros(h_all.shape, F32)
    return _moe_scatter_call(idx.reshape(ne * rows // SCATTER_ROWS, 1, SCATTER_ROWS), ye, acc0)


def _final_kernel(x_ref, acc_ref, gate_ref, g_ref, o_ref):
    x = x_ref[...] + gate_ref[0] * acc_ref[...]
    ms = jnp.mean(x * x, axis=-1, keepdims=True)
    o_ref[...] = x * lax.rsqrt(ms + RMS_EPS) * g_ref[...]


def _final_call(x1, acc, modr, final_g, rows_per_mod, mod_base, row_off):
    tk = x1.shape[0]
    tm = 256
    blk_off = row_off // tm
    return pl.pallas_call(
        _final_kernel,
        out_shape=jax.ShapeDtypeStruct((tk, D_MODEL), F32),
        grid=(tk // tm,),
        in_specs=[pl.BlockSpec((tm, D_MODEL), lambda i: (i, 0)),
                  pl.BlockSpec((tm, D_MODEL), lambda i: (i + blk_off, 0)),
                  pl.BlockSpec((1, 1, D_MODEL), _mod_row_map(tm, rows_per_mod, mod_base, 5)),
                  pl.BlockSpec((1, D_MODEL), lambda i: (0, 0))],
        out_specs=pl.BlockSpec((tm, D_MODEL), lambda i: (i, 0)),
        compiler_params=_cparams(("parallel",), VMEM_LIMIT_MID),
        name="final_norm",
    )(x1, acc, modr, final_g)


def _layer_to_routing(x3, modr, mod_base, rows_per_mod, s0, p, latent, h_all, row_off, total_rows):
    nb, seq, _ = x3.shape
    tk = nb * seq
    x = x3.reshape(tk, D_MODEL)

    z = _inproj_call(x, p['norm1_g'], modr, p['w_in_bf'], rows_per_mod, mod_base)
    zs = _conv_call(z.reshape(nb, seq, D_IN), p['conv_w'], latent).reshape(tk, C_SHIFT)
    lw, a, g = _prep_call(zs, p['w0'], p['w_lora_up'], p['a0'], p['a_up'], p['g_up'])
    y2, s_fin = _wkv_call(zs, lw, a, p['k_k'], p['k_a'], s0, nb, seq)
    ya = _rwkv_out_call(y2, zs, a, g, p['k_a'], p['r_k'], p['ln_x_g'], p['ln_x_b'])
    yb = _sgu_call(z, p['sgu_ln_g'], p['sgu_ln_b'], p['sgu_w'], p['sgu_b3'])
    x1 = _outproj_call(ya, yb, p['w_out_bf'], x, modr, rows_per_mod, mod_base)

    h_all, probs_t = _norm2_call(x1, p['norm2_g'], modr, p['w_router_t'], rows_per_mod, mod_base,
                                 h_all, row_off, total_rows)
    cap = CAPACITY_FACTOR * tk // N_EXPERTS
    idx, gate = _route_call(probs_t, cap)
    return x1, s_fin, h_all, idx.reshape(N_EXPERTS, cap) + row_off, gate.reshape(N_EXPERTS, cap)


def _block_diag_state(s):
    nb = s.shape[0]
    s6 = s.reshape(nb, 2, N_PAIRS_TOTAL, 2, HEAD_A, HEAD_A)
    eye = jnp.eye(2, dtype=s.dtype)
    bd = jnp.einsum('bdpqvk,qr->bdpqvrk', s6, eye)
    return bd.reshape(nb, 2, N_PAIRS_TOTAL, PAIR, PAIR)


def kernel(x_prompt, x_sample, state_wkv, c, c_ctx, norm1_g, norm2_g, w_mod, b_mod, w_in, conv_w, w0, w_lora_up, a0, a_up, g_up, k_k, k_a, r_k, ln_x_g, ln_x_b, sgu_ln_g, sgu_ln_b, sgu_w, sgu_b, w_out, w_router, w_exp_gate, w_exp_up, w_exp_down, final_g):
    depth = w_in.shape[0]
    nb_c = x_prompt.shape[0]
    nb_l = x_sample.shape[0]
    mod_rows = 8
    cvec = jnp.concatenate([c_ctx[None, :], c, jnp.zeros((mod_rows - 1 - nb_l, D_MODEL), F32)], axis=0)

    xc, xl = x_prompt, x_sample
    new_states = []
    for l in range(depth):
        p = {
            'norm1_g': norm1_g[l][None, :], 'norm2_g': norm2_g[l][None, :],
            'w_in_bf': w_in[l].astype(BF16), 'conv_w': conv_w[l],
            'w0': w0[l], 'w_lora_up': w_lora_up[l].reshape(2 * LORA_W, D_A),
            'a0': a0[l], 'a_up': a_up[l].reshape(2 * LORA_A, D_A), 'g_up': g_up[l],
            'k_k': k_k[l][None, :], 'k_a': k_a[l][None, :], 'r_k': r_k[l][None, :],
            'ln_x_g': ln_x_g[l][None, :], 'ln_x_b': ln_x_b[l][None, :],
            'sgu_ln_g': sgu_ln_g[l][None, :], 'sgu_ln_b': sgu_ln_b[l][None, :],
            'sgu_w': sgu_w[l], 'sgu_b3': sgu_b[l][:, :, None],
            'w_out_bf': w_out[l].astype(BF16), 'w_router_t': w_router[l].T,
            'w_exp_gate': w_exp_gate[l], 'w_exp_up': w_exp_up[l], 'w_exp_down': w_exp_down[l],
            'final_g': final_g[None, :],
        }
        mod = _mod_call(cvec, w_mod[l], b_mod[l][None, :])
        modr = mod.reshape(mod_rows * N_MOD, 1, D_MODEL)

        tk_c = nb_c * xc.shape[1]
        tk_l = nb_l * xl.shape[1]
        total = tk_c + tk_l
        s0_c = jnp.zeros((nb_c, 2, N_PAIRS_TOTAL, PAIR, PAIR), F32)
        x1c, s_fin, h_all, idx_c, gate_c = _layer_to_routing(
            xc, modr, 0, tk_c, s0_c, p, False, None, 0, total)
        new_states.append(s_fin.astype(x_prompt.dtype))
        s0_l = _block_diag_state(state_wkv[:, l].astype(F32))
        x1l, _, h_all, idx_l, gate_l = _layer_to_routing(
            xl, modr, 1, xl.shape[1], s0_l, p, True, h_all, tk_c, total)

        acc = _moe_call(jnp.concatenate([idx_c, idx_l], axis=1), jnp.concatenate([gate_c, gate_l], axis=1),
                        h_all, p['w_exp_gate'], p['w_exp_up'], p['w_exp_down'])
        xc = _final_call(x1c, acc, modr, p['final_g'], tk_c, 0, 0).reshape(xc.shape)
        xl = _final_call(x1l, acc, modr, p['final_g'], xl.shape[1], 1, tk_c).reshape(xl.shape)

    assert depth == 1
    state_new = jnp.stack(new_states, axis=1)
    return (xc, xl, state_new)
```

```python
import functools

import jax
import jax.numpy as jnp
from jax import lax
from jax.experimental import pallas as pl
from jax.experimental.pallas import tpu as pltpu

F32 = jnp.float32
BF16 = jnp.bfloat16

D_MODEL = 4096
D_A = 2048
HEAD_A = 64
H_A = 32
D_B = 2048
G_B = 8
CH_B = 256
SGU_CHUNK = 128
LORA_W = 64
LORA_A = 64
LORA_G = 256
C_SHIFT = 3 * D_A + 2 * LORA_W + 2 * LORA_A + LORA_G
D_IN = C_SHIFT + 2 * D_B
N_EXPERTS = 16
CAPACITY_FACTOR = 2
D_FF = 5632
N_MOD = 6
GRID_W = 64
RMS_EPS = 1e-6
GN_EPS = 64e-5
LN_EPS = 1e-5
DECAY_SCALE = 0.6065306597126334

V7X_LANES = 128
V7X_SUBLANES = 8
V7X_VMEM_BYTES = 64 * 1024 * 1024
VMEM_LIMIT_BIG = 52 * 1024 * 1024
VMEM_LIMIT_MID = 40 * 1024 * 1024

WKV_CHUNK = 64
PAIR = 2 * HEAD_A
WKV_GW = 2048
N_PAIRS_TOTAL = D_A // PAIR


def _cparams(sem, vmem=None):
    return pltpu.CompilerParams(dimension_semantics=sem, vmem_limit_bytes=vmem)


def _bdot(a, b):
    return jnp.dot(a.astype(BF16), b.astype(BF16), preferred_element_type=F32)


def _nt(a, b):
    return lax.dot_general(a.astype(BF16), b.astype(BF16), (((1,), (1,)), ((), ())),
                           preferred_element_type=F32)


def _tn(a, b):
    return lax.dot_general(a.astype(BF16), b.astype(BF16), (((0,), (0,)), ((), ())),
                           preferred_element_type=F32)


def _split2(x):
    hi = x.astype(BF16)
    lo = (x - hi.astype(F32)).astype(BF16)
    return hi, lo


def _split3(x):
    hi = x.astype(BF16)
    r1 = x - hi.astype(F32)
    mid = r1.astype(BF16)
    lo = (r1 - mid.astype(F32)).astype(BF16)
    return hi, mid, lo


def _dot3(a, b):
    ah, al = _split2(a)
    bh, bl = _split2(b)
    d = functools.partial(jnp.dot, preferred_element_type=F32)
    return d(ah, bh) + (d(al, bh) + d(ah, bl))


def _dot_exact_lhs(a_bf, b):
    bh, bm, bl = _split3(b)
    d = functools.partial(jnp.dot, preferred_element_type=F32)
    return d(a_bf, bh) + (d(a_bf, bm) + d(a_bf, bl))


def _dot_exact_rhs(a, b_bf):
    ah, al = _split2(a)
    d = functools.partial(jnp.dot, preferred_element_type=F32)
    return d(ah, b_bf) + d(al, b_bf)


def _sigmoid(x):
    return jax.nn.sigmoid(x)


def _mod_kernel(c_ref, w_ref, b_ref, o_ref):
    cv = c_ref[...]
    s = cv * _sigmoid(cv)
    o_ref[...] = _dot3(s, w_ref[...]) + b_ref[...]


def _mod_call(cvec, w_mod, b_mod):
    rows, d = cvec.shape
    n = w_mod.shape[1]
    tn = 512
    return pl.pallas_call(
        _mod_kernel,
        out_shape=jax.ShapeDtypeStruct((rows, n), F32),
        grid=(n // tn,),
        in_specs=[pl.BlockSpec((rows, d), lambda j: (0, 0)),
                  pl.BlockSpec((d, tn), lambda j: (0, j)),
                  pl.BlockSpec((1, tn), lambda j: (0, j))],
        out_specs=pl.BlockSpec((rows, tn), lambda j: (0, j)),
        compiler_params=_cparams(("parallel",), VMEM_LIMIT_MID),
        name="mod_proj",
    )(cvec, w_mod, b_mod)


def _inproj_kernel(x_ref, g_ref, sc_ref, sh_ref, w_ref, o_ref, h_scr):
    @pl.when(pl.program_id(1) == 0)
    def _():
        x = x_ref[...]
        ms = jnp.mean(x * x, axis=-1, keepdims=True)
        y = x * lax.rsqrt(ms + RMS_EPS) * g_ref[...]
        h_scr[...] = (y * (1.0 + sc_ref[0]) + sh_ref[0]).astype(BF16)

    o_ref[...] = jnp.dot(h_scr[...], w_ref[...], preferred_element_type=F32)


def _mod_row_map(tm, rows_per_mod, mod_base, comp):
    def index_map(i, *_):
        return ((mod_base + (i * tm) // rows_per_mod) * N_MOD + comp, 0, 0)
    return index_map


def _inproj_call(x, norm_g, modr, w_bf, rows_per_mod, mod_base):
    tk, d = x.shape
    n = w_bf.shape[1]
    tm, tn = 512, 768
    return pl.pallas_call(
        _inproj_kernel,
        out_shape=jax.ShapeDtypeStruct((tk, n), F32),
        grid=(tk // tm, n // tn),
        in_specs=[pl.BlockSpec((tm, d), lambda i, j: (i, 0)),
                  pl.BlockSpec((1, d), lambda i, j: (0, 0)),
                  pl.BlockSpec((1, 1, d), _mod_row_map(tm, rows_per_mod, mod_base, 1)),
                  pl.BlockSpec((1, 1, d), _mod_row_map(tm, rows_per_mod, mod_base, 0)),
                  pl.BlockSpec((d, tn), lambda i, j: (0, j))],
        out_specs=pl.BlockSpec((tm, tn), lambda i, j: (i, j)),
        scratch_shapes=[pltpu.VMEM((tm, d), BF16)],
        compiler_params=_cparams(("parallel", "arbitrary"), VMEM_LIMIT_BIG),
        name="in_proj",
    )(x, norm_g, modr, modr, w_bf)


CONV_PAD = 72
CONV_ROWS = 256


def _conv_kernel(z_ref, w_ref, o_ref, ext, *, seq, width, vertical):
    bb = z_ref.shape[0]
    tc = z_ref.shape[2]
    zeros = jnp.zeros((CONV_PAD, tc), F32)
    ext[pl.ds(0, CONV_PAD), :] = zeros
    ext[pl.ds(CONV_PAD + seq, CONV_PAD), :] = zeros
    rows = min(CONV_ROWS, seq)
    col = lax.broadcasted_iota(jnp.int32, (rows, tc), 0)
    for b in range(bb):
        ext[pl.ds(CONV_PAD, seq), :] = z_ref[b]
        for r0 in range(0, seq, rows):
            cpos = (col + r0) % width
            acc = None
            for dj in (-1, 0, 1):
                part = None
                for di in ((-1, 0, 1) if vertical else (0,)):
                    tap = ext[pl.ds(CONV_PAD + r0 + di * width + dj, rows), :]
                    term = tap * w_ref[di + 1, pl.ds(dj + 1, 1), :]
                    part = term if part is None else part + term
                if vertical and dj == -1:
                    part = jnp.where(cpos == 0, 0.0, part)
                elif vertical and dj == 1:
                    part = jnp.where(cpos == width - 1, 0.0, part)
                acc = part if acc is None else acc + part
            o_ref[b, pl.ds(r0, rows), :] = acc


def _conv_call(z3, conv_w, latent):
    nb, seq, _ = z3.shape
    if latent:
        bb, tc, width = 1, 256, GRID_W
    else:
        bb, tc, width = 4, 512, seq
    kern = functools.partial(_conv_kernel, seq=seq, width=width, vertical=latent)
    return pl.pallas_call(
        kern,
        out_shape=jax.ShapeDtypeStruct((nb, seq, C_SHIFT), F32),
        grid=(nb // bb, C_SHIFT // tc),
        in_specs=[pl.BlockSpec((bb, seq, tc), lambda b, j: (b, 0, j)),
                  pl.BlockSpec((3, 3, tc), lambda b, j: (0, 0, j))],
        out_specs=pl.BlockSpec((bb, seq, tc), lambda b, j: (b, 0, j)),
        scratch_shapes=[pltpu.VMEM((seq + 2 * CONV_PAD, tc), F32)],
        compiler_params=_cparams(("parallel", "parallel"), VMEM_LIMIT_MID),
        name="short_conv",
    )(z3, conv_w)


def _prep_kernel(zl_ref, w0_ref, wup_ref, a0_ref, aup_ref, gup_ref, lw_ref, a_ref, g_ref):
    zl = zl_ref[...]
    tm = zl.shape[0]
    wd = jnp.tanh(zl[:, 0:2 * LORA_W])
    ad = zl[:, 2 * LORA_W:2 * LORA_W + 2 * LORA_A]
    gd = _sigmoid(zl[:, 2 * LORA_W + 2 * LORA_A:])
    lane = lax.broadcasted_iota(jnp.int32, (tm, 2 * LORA_W), 1)
    for d in range(2):
        sel = (lane < LORA_W) if d == 0 else (lane >= LORA_W)
        wlin = w0_ref[pl.ds(d, 1), :] + _dot3(jnp.where(sel, wd, 0.0), wup_ref[...])
        lw_ref[d] = -DECAY_SCALE * _sigmoid(wlin)
        alin = a0_ref[pl.ds(d, 1), :] + _dot3(jnp.where(sel, ad, 0.0), aup_ref[...])
        a_ref[d] = _sigmoid(alin)
    g_ref[...] = _dot3(gd, gup_ref[...])


def _prep_call(zs, w0, wup, a0, aup, gup):
    tk = zs.shape[0]
    tm = 256
    lblk = (3 * D_A) // 512
    return pl.pallas_call(
        _prep_kernel,
        out_shape=[jax.ShapeDtypeStruct((2, tk, D_A), F32),
                   jax.ShapeDtypeStruct((2, tk, D_A), F32),
                   jax.ShapeDtypeStruct((tk, D_A), F32)],
        grid=(tk // tm,),
        in_specs=[pl.BlockSpec((tm, 512), lambda i: (i, lblk)),
                  pl.BlockSpec((2, D_A), lambda i: (0, 0)),
                  pl.BlockSpec((2 * LORA_W, D_A), lambda i: (0, 0)),
                  pl.BlockSpec((2, D_A), lambda i: (0, 0)),
                  pl.BlockSpec((2 * LORA_A, D_A), lambda i: (0, 0)),
                  pl.BlockSpec((LORA_G, D_A), lambda i: (0, 0))],
        out_specs=[pl.BlockSpec((2, tm, D_A), lambda i: (0, i, 0)),
                   pl.BlockSpec((2, tm, D_A), lambda i: (0, i, 0)),
                   pl.BlockSpec((tm, D_A), lambda i: (i, 0))],
        compiler_params=_cparams(("parallel",), VMEM_LIMIT_MID),
        name="rwkv_prep",
    )(zs, w0, wup, a0, aup, gup)


def _wkv_kernel(r_ref, k_ref, v_ref, lw_ref, a_ref, kk_ref, ka_ref, s0_ref, y_ref, sout_ref, s_scr):
    d = pl.program_id(1)
    ci = pl.program_id(3)
    nci = pl.num_programs(3)
    C = r_ref.shape[0]
    gw = r_ref.shape[1]
    n_pairs = gw // PAIR

    @pl.when(ci == 0)
    def _():
        s_scr[...] = s0_ref[0, 0]

    rev = d == 1
    ri = lax.broadcasted_iota(jnp.int32, (C, C), 0)
    cj = lax.broadcasted_iota(jnp.int32, (C, C), 1)
    before = jnp.where(rev, cj - ri, ri - cj) >= 0
    tri = jnp.where(before, 1.0, 0.0).astype(BF16)

    lw = lw_ref[0]
    cum = _dot_exact_lhs(tri, lw)
    total = jnp.where(rev, cum[0:1, :], cum[C - 1:C, :])

    r2 = lax.broadcasted_iota(jnp.int32, (2 * C, 2 * C), 0)
    c2 = lax.broadcasted_iota(jnp.int32, (2 * C, 2 * C), 1)
    same = (r2 // C) == (c2 // C)
    dd = jnp.where(rev, (c2 % C) - (r2 % C), (r2 % C) - (c2 % C))
    strict = same & (dd > 0)
    incl = same & (dd >= 0)
    eye = jnp.where(r2 == c2, 1.0, 0.0)
    merge_masks = []
    s = 1
    while s < C:
        merge_masks.append(((r2 // (2 * s)) == (c2 // (2 * s))) & ((r2 // s) != (c2 // s)))
        s *= 2
    lane = lax.broadcasted_iota(jnp.int32, (C, PAIR), 1)
    first = lane < HEAD_A
    hr = lax.broadcasted_iota(jnp.int32, (PAIR, PAIR), 0) // HEAD_A
    hc = lax.broadcasted_iota(jnp.int32, (PAIR, PAIR), 1) // HEAD_A
    head_ones = jnp.where(hr == hc, 1.0, 0.0).astype(BF16)

    def pairs(x):
        return jnp.stack([x[:, p * PAIR:(p + 1) * PAIR] for p in range(n_pairs)], axis=0)

    def stack(x):
        return jnp.concatenate([jnp.where(first, x, 0.0), jnp.where(first, 0.0, x)], axis=1)

    def bnn(x, y):
        return lax.dot_general(x.astype(BF16), y.astype(BF16), (((2,), (1,)), ((0,), (0,))),
                               preferred_element_type=F32)

    def bnt(x, y):
        return lax.dot_general(x.astype(BF16), y.astype(BF16), (((2,), (2,)), ((0,), (0,))),
                               preferred_element_type=F32)

    r = pairs(r_ref[...])
    k = pairs(k_ref[...])
    v = pairs(v_ref[...])
    a = pairs(a_ref[0])
    lwp = pairs(lw)
    cp = pairs(cum)
    tot = pairs(total)
    kkr = k * pairs(kk_ref[...])
    n2 = _dot_exact_rhs((kkr * kkr).reshape(n_pairs * C, PAIR), head_ones).reshape(n_pairs, C, PAIR)
    kk = kkr / jnp.maximum(jnp.sqrt(n2), 1e-12)
    kd = k * (1.0 + (a - 1.0) * pairs(ka_ref[...]))
    b = kk * a
    e_in = jnp.exp(cp)
    e_ex = jnp.exp(cp - lwp)
    e_ng = jnp.exp(-cp)
    e_tot = jnp.exp(tot - cp)
    s_a = stack(kk * e_ex)
    s_r = stack(r * e_in)
    s_b = stack(b * e_ng)
    s_k = stack(kd * e_ng)
    s_kp = stack(kd * e_tot)
    s_bp = stack(b * e_tot)
    s_v = stack(v)

    g = bnt(jnp.concatenate([s_a, s_r], axis=1), jnp.concatenate([s_b, s_k], axis=1))
    n = jnp.where(strict, -g[:, 0:2 * C, 0:2 * C], 0.0)
    m_ak = jnp.where(strict, g[:, 0:2 * C, 2 * C:], 0.0)
    m_rb = jnp.where(incl, g[:, 2 * C:, 0:2 * C], 0.0)
    m_rk = jnp.where(incl, g[:, 2 * C:, 2 * C:], 0.0)

    t = eye + jnp.where(merge_masks[0], n, 0.0)
    for mask in merge_masks[1:]:
        t = t + bnn(t, bnn(jnp.where(mask, n, 0.0), t))

    w = bnn(jnp.concatenate([m_ak, m_rk], axis=1), s_v)
    ta = bnn(t, jnp.concatenate([s_a, w[:, 0:2 * C]], axis=2))
    s_at = ta[:, :, 0:PAIR]
    u_hat = ta[:, :, PAIR:]

    s_old = s_scr[...]
    z = bnt(jnp.concatenate([s_at, s_r], axis=1), s_old)
    u = z[:, 0:2 * C] + u_hat
    y_st = z[:, 2 * C:] + w[:, 2 * C:] - bnn(m_rb, u)
    y = y_st[:, 0:C] + y_st[:, C:]
    decay = jnp.exp(tot)
    for p in range(n_pairs):
        y_ref[0, :, p * PAIR:(p + 1) * PAIR] = y[p]
        s_scr[p] = s_old[p] * decay[p] + _tn(jnp.concatenate([s_v[p], u[p]], axis=0),
                                             jnp.concatenate([s_kp[p], -s_bp[p]], axis=0))

    @pl.when(ci == nci - 1)
    def _():
        for p in range(n_pairs):
            s_pair = s_scr[p]
            sout_ref[0, 0, 2 * p] = s_pair[0:HEAD_A, 0:HEAD_A]
            sout_ref[0, 0, 2 * p + 1] = s_pair[HEAD_A:, HEAD_A:]


def _wkv_call(zs, lw, a, k_k, k_a, s0_bd, nb, seq):
    tk = zs.shape[0]
    C = WKV_CHUNK
    nch = seq // C
    gw = WKV_GW
    ng = D_A // gw
    npair = gw // PAIR
    koff = D_A // gw

    def tokmap(b, d, g, c):
        return b * nch + c + d * (nch - 1 - 2 * c)

    return pl.pallas_call(
        _wkv_kernel,
        out_shape=[jax.ShapeDtypeStruct((2, tk, D_A), F32),
                   jax.ShapeDtypeStruct((nb, 2, H_A, HEAD_A, HEAD_A), F32)],
        grid=(nb, 2, ng, nch),
        in_specs=[pl.BlockSpec((C, gw), lambda b, d, g, c: (tokmap(b, d, g, c), g)),
                  pl.BlockSpec((C, gw), lambda b, d, g, c: (tokmap(b, d, g, c), koff + g)),
                  pl.BlockSpec((C, gw), lambda b, d, g, c: (tokmap(b, d, g, c), 2 * koff + g)),
                  pl.BlockSpec((1, C, gw), lambda b, d, g, c: (d, tokmap(b, d, g, c), g)),
                  pl.BlockSpec((1, C, gw), lambda b, d, g, c: (d, tokmap(b, d, g, c), g)),
                  pl.BlockSpec((1, gw), lambda b, d, g, c: (0, g)),
                  pl.BlockSpec((1, gw), lambda b, d, g, c: (0, g)),
                  pl.BlockSpec((1, 1, npair, PAIR, PAIR), lambda b, d, g, c: (b, d, g, 0, 0))],
        out_specs=[pl.BlockSpec((1, C, gw), lambda b, d, g, c: (d, tokmap(b, d, g, c), g)),
                   pl.BlockSpec((1, 1, 2 * npair, HEAD_A, HEAD_A), lambda b, d, g, c: (b, d, g, 0, 0))],
        scratch_shapes=[pltpu.VMEM((npair, PAIR, PAIR), F32)],
        compiler_params=_cparams(("parallel", "parallel", "parallel", "arbitrary"), VMEM_LIMIT_MID),
        name="wkv_scan",
    )(zs, zs, zs, lw, a, k_k, k_a, s0_bd)


def _rwkv_out_kernel(y_ref, r_ref, k_ref, v_ref, a_ref, g_ref, ka_ref, rk_ref, lg_ref, lb_ref, o_ref):
    gw = r_ref.shape[1]
    hr = lax.broadcasted_iota(jnp.int32, (PAIR, PAIR), 0) // HEAD_A
    hc = lax.broadcasted_iota(jnp.int32, (PAIR, PAIR), 1) // HEAD_A
    head_ones = jnp.where(hr == hc, 1.0, 0.0).astype(BF16)
    for p in range(gw // PAIR):
        sl = slice(p * PAIR, (p + 1) * PAIR)
        y = y_ref[0, :, sl] + y_ref[1, :, sl]
        mean = _dot_exact_rhs(y, head_ones) * (1.0 / HEAD_A)
        yc = y - mean
        var = _dot_exact_rhs(yc * yc, head_ones) * (1.0 / HEAD_A)
        yn = yc * lax.rsqrt(var + GN_EPS) * lg_ref[:, sl] + lb_ref[:, sl]
        r = r_ref[:, sl]
        k = k_ref[:, sl]
        ka = ka_ref[:, sl]
        kd0 = k * (1.0 + (a_ref[0, :, sl] - 1.0) * ka)
        kd1 = k * (1.0 + (a_ref[1, :, sl] - 1.0) * ka)
        dotp = _dot_exact_rhs(r * ((kd0 + kd1) * rk_ref[:, sl]), head_ones)
        o_ref[:, sl] = ((yn + dotp * v_ref[:, sl]) * g_ref[:, sl]).astype(BF16)


def _rwkv_out_call(y2, zs, a, g, k_a, r_k, ln_g, ln_b):
    tk = zs.shape[0]
    tm, gw = 256, 512
    koff = D_A // gw
    vec = pl.BlockSpec((1, gw), lambda i, j: (0, j))
    return pl.pallas_call(
        _rwkv_out_kernel,
        out_shape=jax.ShapeDtypeStruct((tk, D_A), BF16),
        grid=(tk // tm, D_A // gw),
        in_specs=[pl.BlockSpec((2, tm, gw), lambda i, j: (0, i, j)),
                  pl.BlockSpec((tm, gw), lambda i, j: (i, j)),
                  pl.BlockSpec((tm, gw), lambda i, j: (i, koff + j)),
                  pl.BlockSpec((tm, gw), lambda i, j: (i, 2 * koff + j)),
                  pl.BlockSpec((2, tm, gw), lambda i, j: (0, i, j)),
                  pl.BlockSpec((tm, gw), lambda i, j: (i, j)),
                  vec, vec, vec, vec],
        out_specs=pl.BlockSpec((tm, gw), lambda i, j: (i, j)),
        compiler_params=_cparams(("parallel", "parallel"), VMEM_LIMIT_MID),
        name="rwkv_out",
    )(y2, zs, zs, zs, a, g, k_a, r_k, ln_g, ln_b)


def _gelu(x):
    return 0.5 * x * (1.0 + lax.erf(x * 0.7071067811865476))


def _sgu_kernel(*refs):
    nblk = (len(refs) - 5) // 2
    zu_refs, zv_refs = refs[:nblk], refs[nblk:2 * nblk]
    lg_ref, lb_ref, ws_ref, bs_ref, o_ref = refs[2 * nblk:]
    tm = o_ref.shape[0]
    per_blk = zu_refs[0].shape[1] // CH_B
    for grp in range(G_B):
        zu_ref, zv_ref = zu_refs[grp // per_blk], zv_refs[grp // per_blk]
        bsl = slice((grp % per_blk) * CH_B, (grp % per_blk + 1) * CH_B)
        sl = slice(grp * CH_B, (grp + 1) * CH_B)
        vf = _gelu(zv_ref[:, bsl])
        mean = jnp.mean(vf, axis=-1, keepdims=True)
        vc = vf - mean
        var = jnp.mean(vc * vc, axis=-1, keepdims=True)
        vn = vc * lax.rsqrt(var + LN_EPS) * lg_ref[:, sl] + lb_ref[:, sl]
        for c0 in range(0, tm, SGU_CHUNK):
            mixed = _bdot(ws_ref[grp], vn[c0:c0 + SGU_CHUNK]) + bs_ref[grp]
            u = _gelu(zu_ref[pl.ds(c0, SGU_CHUNK), bsl])
            o_ref[pl.ds(c0, SGU_CHUNK), sl] = (u * mixed).astype(BF16)


def _sgu_call(z, ln_g, ln_b, w_s, b_s3):
    tk = z.shape[0]
    tm = 256
    bw = 512
    u0 = C_SHIFT // bw
    v0 = (C_SHIFT + D_B) // bw
    nbw = D_B // bw
    in_specs = ([pl.BlockSpec((tm, bw), functools.partial(lambda i, q: (i, q), q=u0 + q)) for q in range(nbw)]
                + [pl.BlockSpec((tm, bw), functools.partial(lambda i, q: (i, q), q=v0 + q)) for q in range(nbw)]
                + [pl.BlockSpec((1, D_B), lambda i: (0, 0)),
                   pl.BlockSpec((1, D_B), lambda i: (0, 0)),
                   pl.BlockSpec((G_B, SGU_CHUNK, SGU_CHUNK), lambda i: (0, 0, 0)),
                   pl.BlockSpec((G_B, SGU_CHUNK, 1), lambda i: (0, 0, 0))])
    return pl.pallas_call(
        _sgu_kernel,
        out_shape=jax.ShapeDtypeStruct((tk, D_B), BF16),
        grid=(tk // tm,),
        in_specs=in_specs,
        out_specs=pl.BlockSpec((tm, D_B), lambda i: (i, 0)),
        compiler_params=_cparams(("parallel",), VMEM_LIMIT_MID),
        name="sgu",
    )(*([z] * (2 * nbw)), ln_g, ln_b, w_s, b_s3)


def _outproj_kernel(ya_ref, yb_ref, wa_ref, wb_ref, x_ref, gate_ref, o_ref):
    acc = jnp.dot(ya_ref[...], wa_ref[...], preferred_element_type=F32)
    acc = acc + jnp.dot(yb_ref[...], wb_ref[...], preferred_element_type=F32)
    o_ref[...] = x_ref[...] + gate_ref[0] * acc


def _outproj_call(ya, yb, w_bf, x, modr, rows_per_mod, mod_base):
    tk = x.shape[0]
    tm, tn = 512, 1024

    def gate_map(i, j):
        return ((mod_base + (i * tm) // rows_per_mod) * N_MOD + 2, 0, j)

    return pl.pallas_call(
        _outproj_kernel,
        out_shape=jax.ShapeDtypeStruct((tk, D_MODEL), F32),
        grid=(tk // tm, D_MODEL // tn),
        in_specs=[pl.BlockSpec((tm, D_A), lambda i, j: (i, 0)),
                  pl.BlockSpec((tm, D_B), lambda i, j: (i, 0)),
                  pl.BlockSpec((D_A, tn), lambda i, j: (0, j)),
                  pl.BlockSpec((D_B, tn), lambda i, j: (1, j)),
                  pl.BlockSpec((tm, tn), lambda i, j: (i, j)),
                  pl.BlockSpec((1, 1, tn), gate_map)],
        out_specs=pl.BlockSpec((tm, tn), lambda i, j: (i, j)),
        compiler_params=_cparams(("parallel", "arbitrary"), VMEM_LIMIT_MID),
        name="out_proj",
    )(ya, yb, w_bf, w_bf, x, modr)


def _norm2_kernel(x_ref, g_ref, sc_ref, sh_ref, wr_ref, *rest):
    h_ref, p_ref = rest[-2:]
    x = x_ref[...]
    ms = jnp.mean(x * x, axis=-1, keepdims=True)
    y = x * lax.rsqrt(ms + RMS_EPS) * g_ref[...]
    h = y * (1.0 + sc_ref[0]) + sh_ref[0]
    h_ref[...] = h
    hh, hl = _split2(h)
    wh, wl = _split2(wr_ref[...])
    nt = functools.partial(lax.dot_general, dimension_numbers=(((1,), (1,)), ((), ())),
                           preferred_element_type=F32)
    logits = nt(wh, hh) + (nt(wh, hl) + nt(wl, hh))
    m = jnp.max(logits, axis=0, keepdims=True)
    e = jnp.exp(logits - m)
    p_ref[...] = e / jnp.sum(e, axis=0, keepdims=True)


def _norm2_call(x1, norm_g, modr, w_router_t, rows_per_mod, mod_base, h_all, row_off, total_rows):
    tk = x1.shape[0]
    tm = 256
    blk_off = row_off // tm
    in_specs = [pl.BlockSpec((tm, D_MODEL), lambda i: (i, 0)),
                pl.BlockSpec((1, D_MODEL), lambda i: (0, 0)),
                pl.BlockSpec((1, 1, D_MODEL), _mod_row_map(tm, rows_per_mod, mod_base, 4)),
                pl.BlockSpec((1, 1, D_MODEL), _mod_row_map(tm, rows_per_mod, mod_base, 3)),
                pl.BlockSpec((N_EXPERTS, D_MODEL), lambda i: (0, 0))]
    args = [x1, norm_g, modr, modr, w_router_t]
    aliases = {}
    if h_all is not None:
        in_specs.append(pl.BlockSpec(memory_space=pl.ANY))
        args.append(h_all)
        aliases = {5: 0}
    return pl.pallas_call(
        _norm2_kernel,
        out_shape=[jax.ShapeDtypeStruct((total_rows, D_MODEL), F32),
                   jax.ShapeDtypeStruct((N_EXPERTS, tk), F32)],
        grid=(tk // tm,),
        in_specs=in_specs,
        out_specs=[pl.BlockSpec((tm, D_MODEL), lambda i: (i + blk_off, 0)),
                   pl.BlockSpec((N_EXPERTS, tm), lambda i: (0, i))],
        input_output_aliases=aliases,
        compiler_params=_cparams(("parallel",), VMEM_LIMIT_MID),
        name="norm2_router",
    )(*args)


ROUTE_SLOTS = 128


def _route_kernel(p_ref, cnt_ref, slot_ref, *, cap):
    ne, nt = p_ref.shape
    cnt_scr, sel_scr = cnt_ref, slot_ref
    p = p_ref[...]
    bits = pltpu.bitcast(p, jnp.int32)

    def count_ge(v):
        return jnp.sum(jnp.where(bits >= v, 1.0, 0.0), axis=1, keepdims=True)

    def bis(it, cur):
        cand = cur | (jnp.int32(1) << (30 - it))
        return jnp.where(count_ge(cand) >= cap, cand, cur)

    thr = lax.fori_loop(0, 31, bis, jnp.zeros((ne, 1), jnp.int32))
    gt = bits > thr
    eq = bits == thr
    need = cap - jnp.sum(jnp.where(gt, 1.0, 0.0), axis=1, keepdims=True)

    lr = lax.broadcasted_iota(jnp.int32, (V7X_LANES, V7X_LANES), 0)
    lc = lax.broadcasted_iota(jnp.int32, (V7X_LANES, V7X_LANES), 1)
    upper = jnp.where(lr <= lc, 1.0, 0.0).astype(BF16)

    def prefix(mask_f32, dst):
        carry = jnp.zeros((ne, 1), F32)
        for c0 in range(0, nt, V7X_LANES):
            blk = jnp.dot(mask_f32[:, c0:c0 + V7X_LANES].astype(BF16), upper,
                          preferred_element_type=F32) + carry
            dst[:, c0:c0 + V7X_LANES] = blk
            carry = blk[:, V7X_LANES - 1:V7X_LANES]

    prefix(jnp.where(eq, 1.0, 0.0), cnt_scr)
    sel = gt | (eq & (cnt_scr[...] <= need))
    prefix(jnp.where(sel, 1.0, 0.0), cnt_scr)
    sel_scr[...] = jnp.where(sel, cnt_scr[...], -1.0)


def _slots_kernel(ends_ref, cnt_ref, slot_ref, p_ref, idx_ref, gate_ref):
    nt = cnt_ref.shape[2]
    nch = nt // V7X_LANES
    s0 = pl.program_id(1) * ROUTE_SLOTS
    slot = (lax.broadcasted_iota(jnp.int32, (ROUTE_SLOTS, V7X_LANES), 0) + s0).astype(F32)

    def bounds(c, carry):
        lo, hi = carry
        end_c = ends_ref[0, 0, c]
        prev_c = jnp.where(c > 0, ends_ref[0, 0, jnp.maximum(c - 1, 0)], 0)
        lo = lo + jnp.where(end_c <= s0, 1, 0)
        hi = hi + jnp.where(prev_c <= s0 + ROUTE_SLOTS - 1, 1, 0)
        return lo, hi

    lo, hi = lax.fori_loop(0, nch, bounds, (jnp.int32(0), jnp.int32(0)))

    def chunk(c, accs):
        acc_i, acc_g = accs
        off = pl.multiple_of(c * V7X_LANES, V7X_LANES)
        c_row = cnt_ref[0, :, pl.ds(off, V7X_LANES)]
        m_row = slot_ref[0, :, pl.ds(off, V7X_LANES)]
        p_row = p_ref[0, :, pl.ds(off, V7X_LANES)]
        acc_i = acc_i + jnp.where(c_row <= slot, 1.0, 0.0)
        acc_g = acc_g + jnp.where(m_row == slot + 1.0, p_row, 0.0)
        return acc_i, acc_g

    zero = jnp.zeros((ROUTE_SLOTS, V7X_LANES), F32)
    acc_i, acc_g = lax.fori_loop(lo, hi, chunk, (zero, zero))
    before = (lo * V7X_LANES).astype(F32)
    idx_ref[0] = (jnp.sum(acc_i, axis=1, keepdims=True) + before).astype(jnp.int32)
    gate_ref[0] = jnp.sum(acc_g, axis=1, keepdims=True)


def _route_call(probs_t, cap):
    ne, nt = probs_t.shape
    assert cap % ROUTE_SLOTS == 0 and nt % V7X_LANES == 0
    cnt, slot = pl.pallas_call(
        functools.partial(_route_kernel, cap=cap),
        out_shape=[jax.ShapeDtypeStruct((ne, nt), F32), jax.ShapeDtypeStruct((ne, nt), F32)],
        compiler_params=_cparams(None, VMEM_LIMIT_MID),
        name="route_select",
    )(probs_t)
    nch = nt // V7X_LANES
    ends = cnt.reshape(ne, nch, V7X_LANES)[:, :, V7X_LANES - 1].astype(jnp.int32)
    row = pl.BlockSpec((1, 1, nt), lambda e, s: (e, 0, 0))
    return pl.pallas_call(
        _slots_kernel,
        out_shape=[jax.ShapeDtypeStruct((ne, cap, 1), jnp.int32),
                   jax.ShapeDtypeStruct((ne, cap, 1), F32)],
        grid=(ne, cap // ROUTE_SLOTS),
        in_specs=[pl.BlockSpec((1, 1, nch), lambda e, s: (e, 0, 0), memory_space=pltpu.SMEM), row, row, row],
        out_specs=[pl.BlockSpec((1, ROUTE_SLOTS, 1), lambda e, s: (e, s, 0)),
                   pl.BlockSpec((1, ROUTE_SLOTS, 1), lambda e, s: (e, s, 0))],
        compiler_params=_cparams(("parallel", "parallel"), VMEM_LIMIT_MID),
        name="route_slots",
    )(ends.reshape(ne, 1, nch), cnt.reshape(ne, 1, nt), slot.reshape(ne, 1, nt), probs_t.reshape(ne, 1, nt))


EXPERT_TF = 256
EXPERT_TN = 256
GATHER_ROWS = 256
SCATTER_ROWS = 512
DMA_ISSUE_UNROLL = 8


def _row_copy(src, si, dst, di, sem):
    return pltpu.make_async_copy(src.at[pl.ds(si, 1), :], dst.at[pl.ds(di, 1), :], sem)


def _moe_up_kernel(idx_ref, h_hbm, wg_ref, wu_ref, o_ref, xe, stage, sems):
    rows = xe.shape[0]
    sub = stage.shape[1]
    nbatch = rows // sub

    @pl.when(pl.program_id(1) == 0)
    def _():
        def start(bi):
            slot = bi % 2

            def issue(j, carry):
                _row_copy(h_hbm, idx_ref[0, 0, bi * sub + j], stage.at[slot], j, sems.at[slot]).start()
                return carry
            lax.fori_loop(0, sub, issue, 0, unroll=DMA_ISSUE_UNROLL)

        start(0)
        for bi in range(nbatch):
            slot = bi % 2
            if bi + 1 < nbatch:
                start(bi + 1)
            pltpu.make_async_copy(h_hbm.at[pl.ds(0, sub), :], stage.at[slot], sems.at[slot]).wait()
            xe[pl.ds(bi * sub, sub), :] = stage[slot].astype(BF16)

    x = xe[...]
    hg = jnp.dot(x, wg_ref[0].astype(BF16), preferred_element_type=F32)
    hu = jnp.dot(x, wu_ref[0].astype(BF16), preferred_element_type=F32)
    o_ref[...] = (hg * _sigmoid(hg) * hu).astype(BF16)


def _moe_up_call(idx3, h_all, w_gate, w_up):
    rows = idx3.shape[2]
    tf = EXPERT_TF
    assert rows % GATHER_ROWS == 0
    return pl.pallas_call(
        _moe_up_kernel,
        out_shape=jax.ShapeDtypeStruct((N_EXPERTS * rows, D_FF), BF16),
        grid=(N_EXPERTS, D_FF // tf),
        in_specs=[pl.BlockSpec((1, 1, rows), lambda e, f: (e, 0, 0), memory_space=pltpu.SMEM),
                  pl.BlockSpec(memory_space=pl.ANY),
                  pl.BlockSpec((1, D_MODEL, tf), lambda e, f: (e, 0, f)),
                  pl.BlockSpec((1, D_MODEL, tf), lambda e, f: (e, 0, f))],
        out_specs=pl.BlockSpec((rows, tf), lambda e, f: (e, f)),
        scratch_shapes=[pltpu.VMEM((rows, D_MODEL), BF16),
                        pltpu.VMEM((2, GATHER_ROWS, D_MODEL), F32),
                        pltpu.SemaphoreType.DMA((2,))],
        compiler_params=_cparams(("arbitrary", "arbitrary"), VMEM_LIMIT_BIG),
        name="moe_up",
    )(idx3, h_all, w_gate, w_up)


def _moe_down_kernel(gate_ref, hh_hbm, wd_ref, o_ref, hbuf, sem):
    rows = hbuf.shape[0]

    @pl.when(pl.program_id(1) == 0)
    def _():
        src = hh_hbm.at[pl.ds(pl.program_id(0) * rows, rows), :]
        cp = pltpu.make_async_copy(src, hbuf, sem)
        cp.start()
        cp.wait()

    o_ref[...] = jnp.dot(hbuf[...], wd_ref[0].astype(BF16), preferred_element_type=F32) * gate_ref[0]


def _moe_down_call(gate3, hh, w_down):
    rows = gate3.shape[1]
    tn = EXPERT_TN
    return pl.pallas_call(
        _moe_down_kernel,
        out_shape=jax.ShapeDtypeStruct((N_EXPERTS * rows, D_MODEL), F32),
        grid=(N_EXPERTS, D_MODEL // tn),
        in_specs=[pl.BlockSpec((1, rows, 1), lambda e, n: (e, 0, 0)),
                  pl.BlockSpec(memory_space=pl.ANY),
                  pl.BlockSpec((1, D_FF, tn), lambda e, n: (e, 0, n))],
        out_specs=pl.BlockSpec((rows, tn), lambda e, n: (e, n)),
        scratch_shapes=[pltpu.VMEM((rows, D_FF), BF16), pltpu.SemaphoreType.DMA],
        compiler_params=_cparams(("arbitrary", "arbitrary"), VMEM_LIMIT_BIG),
        name="moe_down",
    )(gate3, hh, w_down)


def _moe_scatter_kernel(idx_ref, nxt_ref, ye_ref, acc_in, acc_hbm, rbuf, sem_r, sem_w, *, per_expert):
    del acc_in
    i = pl.program_id(0)
    n = pl.num_programs(0)
    rows = rbuf.shape[1]
    slot = i % 2

    def read_rows(ref, dst):
        def issue(j, carry):
            _row_copy(acc_hbm, ref[0, 0, j], rbuf.at[dst], j, sem_r.at[dst]).start()
            return carry
        lax.fori_loop(0, rows, issue, 0, unroll=DMA_ISSUE_UNROLL)

    @pl.when(i == 0)
    def _():
        read_rows(idx_ref, 0)

    pltpu.make_async_copy(acc_hbm.at[pl.ds(0, rows), :], rbuf.at[slot], sem_r.at[slot]).wait()
    has_next = i + 1 < n
    next_starts_expert = (i + 1) % per_expert == 0

    @pl.when(has_next & jnp.logical_not(next_starts_expert))
    def _():
        read_rows(nxt_ref, 1 - slot)

    rbuf[slot] = rbuf[slot] + ye_ref[...]

    def issue_w(j, carry):
        _row_copy(rbuf.at[slot], j, acc_hbm, idx_ref[0, 0, j], sem_w).start()
        return carry
    lax.fori_loop(0, rows, issue_w, 0, unroll=DMA_ISSUE_UNROLL)
    pltpu.make_async_copy(rbuf.at[slot], acc_hbm.at[pl.ds(0, rows), :], sem_w).wait()

    @pl.when(has_next & next_starts_expert)
    def _():
        read_rows(nxt_ref, 1 - slot)


def _moe_scatter_call(idx_blocks, ye, acc, per_expert):
    nblk, _, rows = idx_blocks.shape
    idx_spec = functools.partial(pl.BlockSpec, (1, 1, rows), memory_space=pltpu.SMEM)
    return pl.pallas_call(
        functools.partial(_moe_scatter_kernel, per_expert=per_expert),
        out_shape=jax.ShapeDtypeStruct(acc.shape, F32),
        grid=(nblk,),
        in_specs=[idx_spec(lambda i: (i, 0, 0)),
                  idx_spec(lambda i: (jnp.minimum(i + 1, nblk - 1), 0, 0)),
                  pl.BlockSpec((rows, D_MODEL), lambda i: (i, 0)),
                  pl.BlockSpec(memory_space=pl.ANY)],
        out_specs=pl.BlockSpec(memory_space=pl.ANY),
        scratch_shapes=[pltpu.VMEM((2, rows, D_MODEL), F32),
                        pltpu.SemaphoreType.DMA((2,)),
                        pltpu.SemaphoreType.DMA],
        input_output_aliases={3: 0},
        compiler_params=_cparams(("arbitrary",), VMEM_LIMIT_BIG),
        name="moe_scatter",
    )(idx_blocks, idx_blocks, ye, acc)


def _moe_call(idx, gate, h_all, w_gate, w_up, w_down):
    ne, rows = idx.shape
    assert rows % SCATTER_ROWS == 0
    hh = _moe_up_call(idx.reshape(ne, 1, rows), h_all, w_gate, w_up)
    ye = _moe_down_call(gate.reshape(ne, rows, 1), hh, w_down)
    acc0 = jnp.zeros(h_all.shape, F32)
    return _moe_scatter_call(idx.reshape(ne * rows // SCATTER_ROWS, 1, SCATTER_ROWS), ye, acc0,
                             rows // SCATTER_ROWS)


def _final_kernel(x_ref, acc_ref, gate_ref, g_ref, o_ref):
    x = x_ref[...] + gate_ref[0] * acc_ref[...]
    ms = jnp.mean(x * x, axis=-1, keepdims=True)
    o_ref[...] = x * lax.rsqrt(ms + RMS_EPS) * g_ref[...]


def _final_call(x1, acc, modr, final_g, rows_per_mod, mod_base, row_off):
    tk = x1.shape[0]
    tm = 256
    blk_off = row_off // tm
    return pl.pallas_call(
        _final_kernel,
        out_shape=jax.ShapeDtypeStruct((tk, D_MODEL), F32),
        grid=(tk // tm,),
        in_specs=[pl.BlockSpec((tm, D_MODEL), lambda i: (i, 0)),
                  pl.BlockSpec((tm, D_MODEL), lambda i: (i + blk_off, 0)),
                  pl.BlockSpec((1, 1, D_MODEL), _mod_row_map(tm, rows_per_mod, mod_base, 5)),
                  pl.BlockSpec((1, D_MODEL), lambda i: (0, 0))],
        out_specs=pl.BlockSpec((tm, D_MODEL), lambda i: (i, 0)),
        compiler_params=_cparams(("parallel",), VMEM_LIMIT_MID),
        name="final_norm",
    )(x1, acc, modr, final_g)


def _layer_to_routing(x3, modr, mod_base, rows_per_mod, s0, p, latent, h_all, row_off, total_rows):
    nb, seq, _ = x3.shape
    tk = nb * seq
    x = x3.reshape(tk, D_MODEL)

    z = _inproj_call(x, p['norm1_g'], modr, p['w_in_bf'], rows_per_mod, mod_base)
    zs = _conv_call(z.reshape(nb, seq, D_IN), p['conv_w'], latent).reshape(tk, C_SHIFT)
    lw, a, g = _prep_call(zs, p['w0'], p['w_lora_up'], p['a0'], p['a_up'], p['g_up'])
    y2, s_fin = _wkv_call(zs, lw, a, p['k_k'], p['k_a'], s0, nb, seq)
    ya = _rwkv_out_call(y2, zs, a, g, p['k_a'], p['r_k'], p['ln_x_g'], p['ln_x_b'])
    yb = _sgu_call(z, p['sgu_ln_g'], p['sgu_ln_b'], p['sgu_w'], p['sgu_b3'])
    x1 = _outproj_call(ya, yb, p['w_out_bf'], x, modr, rows_per_mod, mod_base)

    h_all, probs_t = _norm2_call(x1, p['norm2_g'], modr, p['w_router_t'], rows_per_mod, mod_base,
                                 h_all, row_off, total_rows)
    cap = CAPACITY_FACTOR * tk // N_EXPERTS
    idx, gate = _route_call(probs_t, cap)
    return x1, s_fin, h_all, idx.reshape(N_EXPERTS, cap) + row_off, gate.reshape(N_EXPERTS, cap)


def _block_diag_state(s):
    nb = s.shape[0]
    s6 = s.reshape(nb, 2, N_PAIRS_TOTAL, 2, HEAD_A, HEAD_A)
    eye = jnp.eye(2, dtype=s.dtype)
    bd = jnp.einsum('bdpqvk,qr->bdpqvrk', s6, eye)
    return bd.reshape(nb, 2, N_PAIRS_TOTAL, PAIR, PAIR)


def kernel(x_prompt, x_sample, state_wkv, c, c_ctx, norm1_g, norm2_g, w_mod, b_mod, w_in, conv_w, w0, w_lora_up, a0, a_up, g_up, k_k, k_a, r_k, ln_x_g, ln_x_b, sgu_ln_g, sgu_ln_b, sgu_w, sgu_b, w_out, w_router, w_exp_gate, w_exp_up, w_exp_down, final_g):
    depth = w_in.shape[0]
    nb_c = x_prompt.shape[0]
    nb_l = x_sample.shape[0]
    mod_rows = 8
    cvec = jnp.concatenate([c_ctx[None, :], c, jnp.zeros((mod_rows - 1 - nb_l, D_MODEL), F32)], axis=0)

    xc, xl = x_prompt, x_sample
    new_states = []
    for l in range(depth):
        p = {
            'norm1_g': norm1_g[l][None, :], 'norm2_g': norm2_g[l][None, :],
            'w_in_bf': w_in[l].astype(BF16), 'conv_w': conv_w[l],
            'w0': w0[l], 'w_lora_up': w_lora_up[l].reshape(2 * LORA_W, D_A),
            'a0': a0[l], 'a_up': a_up[l].reshape(2 * LORA_A, D_A), 'g_up': g_up[l],
            'k_k': k_k[l][None, :], 'k_a': k_a[l][None, :], 'r_k': r_k[l][None, :],
            'ln_x_g': ln_x_g[l][None, :], 'ln_x_b': ln_x_b[l][None, :],
            'sgu_ln_g': sgu_ln_g[l][None, :], 'sgu_ln_b': sgu_ln_b[l][None, :],
            'sgu_w': sgu_w[l], 'sgu_b3': sgu_b[l][:, :, None],
            'w_out_bf': w_out[l].astype(BF16), 'w_router_t': w_router[l].T,
            'w_exp_gate': w_exp_gate[l], 'w_exp_up': w_exp_up[l], 'w_exp_down': w_exp_down[l],
            'final_g': final_g[None, :],
        }
        mod = _mod_call(cvec, w_mod[l], b_mod[l][None, :])
        modr = mod.reshape(mod_rows * N_MOD, 1, D_MODEL)

        tk_c = nb_c * xc.shape[1]
        tk_l = nb_l * xl.shape[1]
        total = tk_c + tk_l
        s0_c = jnp.zeros((nb_c, 2, N_PAIRS_TOTAL, PAIR, PAIR), F32)
        x1c, s_fin, h_all, idx_c, gate_c = _layer_to_routing(
            xc, modr, 0, tk_c, s0_c, p, False, None, 0, total)
        new_states.append(s_fin.astype(x_prompt.dtype))
        s0_l = _block_diag_state(state_wkv[:, l].astype(F32))
        x1l, _, h_all, idx_l, gate_l = _layer_to_routing(
            xl, modr, 1, xl.shape[1], s0_l, p, True, h_all, tk_c, total)

        acc = _moe_call(jnp.concatenate([idx_c, idx_l], axis=1), jnp.concatenate([gate_c, gate_l], axis=1),
                        h_all, p['w_exp_gate'], p['w_exp_up'], p['w_exp_down'])
        xc = _final_call(x1c, acc, modr, p['final_g'], tk_c, 0, 0).reshape(xc.shape)
        xl = _final_call(x1l, acc, modr, p['final_g'], xl.shape[1], 1, tk_c).reshape(xl.shape)

    assert depth == 1
    state_new = jnp.stack(new_states, axis=1)
    return (xc, xl, state_new)
```
